```python
import jax, jax.numpy as jnp
from jax import lax
import numpy as np

D_MODEL = 1024
BATCH = 4
SEQ = 4096
DEPTH = 1
DEC_BATCH = 128
DEC_SEQ = 1
PAST_LEN = 8192
PAGE_SIZE = 128

MIX_W = D_MODEL
ATT_W = MIX_W // 2
CONV_W = MIX_W - ATT_W
HEAD_DIM = 64
ATT_HEADS = ATT_W // HEAD_DIM
CONV_GROUPS = CONV_W // HEAD_DIM
CONV_WIDTH = 3
PLE_DIM = 256
D_FF = -(-8 * D_MODEL // (3 * 256)) * 256
Q_BLOCK = 128
EPS = 1e-6
BIAS_HI = 6.0
BIAS_LO = 11.0

kernel_name = "hymba_stickbreak_shortconv_step"


def rmsnorm(x, g):
    xf = x.astype(jnp.float32)
    y = xf * lax.rsqrt(jnp.mean(xf * xf, axis=-1, keepdims=True) + EPS)
    return (y * g.astype(jnp.float32)).astype(x.dtype)


def stick_breaking(q, k, v, att_bias, q_pos, k_pos):
    z = jnp.einsum("bqhd,bkhd->bhqk", q, k,
                   preferred_element_type=jnp.float32) * (HEAD_DIM ** -0.5)
    z = z + att_bias.astype(jnp.float32)[None, :, None, None]
    causal = k_pos[None, :] < q_pos[:, None]
    log_beta = jax.nn.log_sigmoid(z)
    log_rest = jnp.where(causal, jax.nn.log_sigmoid(-z), 0.0)
    later = lax.cumsum(log_rest, axis=3, reverse=True) - log_rest
    w = jnp.where(causal, jnp.exp(log_beta + later), 0.0)
    return jnp.einsum("bhqk,bkhd->bqhd", w.astype(v.dtype), v)


def short_conv(u, buf, conv_w):
    t = u.shape[1]
    up = jnp.concatenate([buf.astype(u.dtype), u], axis=1)
    y = conv_w[0] * up[:, 0:t] + conv_w[1] * up[:, 1:t + 1] + conv_w[2] * up[:, 2:t + 2]
    return y, up[:, -(CONV_WIDTH - 1):]


def in_projection(h_n, w_in, q_norm, k_norm):
    b, t, _ = h_n.shape
    proj = h_n @ w_in
    cuts = [ATT_W, 2 * ATT_W, 3 * ATT_W, 3 * ATT_W + CONV_W, 3 * ATT_W + 2 * CONV_W]
    q, k, v, b_gate, c_gate, hc = jnp.split(proj, cuts, axis=-1)
    q = rmsnorm(q.reshape(b, t, ATT_HEADS, HEAD_DIM), q_norm)
    k = rmsnorm(k.reshape(b, t, ATT_HEADS, HEAD_DIM), k_norm)
    v = v.reshape(b, t, ATT_HEADS, HEAD_DIM)
    return q, k, v, b_gate, c_gate, hc


def post_mixer(x, attn_o, conv_y, b_gate, p_i, attn_out_norm, conv_out_norm, w_out,
               norm_ffn, w_gate, w_up, w_down, ple_norm, w_ple_gate, w_ple_proj):
    b, t, _ = x.shape
    mix = jnp.concatenate([rmsnorm(attn_o.reshape(b, t, ATT_W), attn_out_norm),
                           rmsnorm(b_gate * conv_y, conv_out_norm)], axis=-1)
    h = x + mix @ w_out
    hn = rmsnorm(h, norm_ffn)
    h = h + (jax.nn.silu(hn @ w_gate) * (hn @ w_up)) @ w_down
    gate = jax.nn.sigmoid(rmsnorm(h, ple_norm) @ w_ple_gate)
    return h + (p_i @ w_ple_proj) * gate


def setup_inputs(seed: int = 0) -> dict:
    key = jax.random.key(seed)
    ks = jax.random.split(key, 26)
    f32 = jnp.float32

    def nrm(k, shape, scale):
        return jax.random.normal(k, shape, f32) * scale

    def gain(k, shape):
        return 1.0 + 0.05 * jax.random.normal(k, shape, f32)

    n_pages = PAST_LEN // PAGE_SIZE
    n_used = DEC_BATCH * n_pages
    n_pool = n_used + n_used // 4
    page_table = jax.random.permutation(ks[0], n_pool)[:n_used].reshape(
        DEC_BATCH, n_pages).astype(jnp.int32)
    att_bias = (-jnp.linspace(BIAS_HI, BIAS_LO, ATT_HEADS, dtype=f32)[None, :]
                + 0.1 * jax.random.normal(ks[23], (DEPTH, ATT_HEADS), f32))
    return {
        "x_prompt": nrm(ks[1], (BATCH, SEQ, D_MODEL), 1.0),
        "x_sample": nrm(ks[2], (DEC_BATCH, DEC_SEQ, D_MODEL), 1.0),
        "cache_k": nrm(ks[3], (DEPTH, n_pool, PAGE_SIZE, ATT_HEADS, HEAD_DIM), 1.0),
        "cache_v": nrm(ks[4], (DEPTH, n_pool, PAGE_SIZE, ATT_HEADS, HEAD_DIM), 1.0),
        "state_conv": nrm(ks[5], (DEPTH, DEC_BATCH, CONV_WIDTH - 1, CONV_W), 1.0),
        "page_table": page_table,
        "p_prompt": nrm(ks[6], (DEPTH, BATCH, SEQ, PLE_DIM), 1.0),
        "p_sample": nrm(ks[7], (DEPTH, DEC_BATCH, DEC_SEQ, PLE_DIM), 1.0),
        "norm_mix": gain(ks[8], (DEPTH, D_MODEL)),
        "w_in": nrm(ks[9], (DEPTH, D_MODEL, 3 * ATT_W + 3 * CONV_W), D_MODEL ** -0.5),
        "q_norm": gain(ks[10], (DEPTH, HEAD_DIM)),
        "k_norm": gain(ks[11], (DEPTH, HEAD_DIM)),
        "att_bias": att_bias,
        "conv_w": nrm(ks[12], (DEPTH, CONV_WIDTH, CONV_W), CONV_WIDTH ** -0.5),
        "attn_out_norm": gain(ks[13], (DEPTH, ATT_W)),
        "conv_out_norm": gain(ks[14], (DEPTH, CONV_W)),
        "w_out": nrm(ks[15], (DEPTH, MIX_W, D_MODEL), MIX_W ** -0.5),
        "norm_ffn": gain(ks[16], (DEPTH, D_MODEL)),
        "w_gate": nrm(ks[17], (DEPTH, D_MODEL, D_FF), D_MODEL ** -0.5),
        "w_up": nrm(ks[18], (DEPTH, D_MODEL, D_FF), D_MODEL ** -0.5),
        "w_down": nrm(ks[19], (DEPTH, D_FF, D_MODEL), D_FF ** -0.5),
        "ple_norm": gain(ks[20], (DEPTH, D_MODEL)),
        "w_ple_gate": nrm(ks[21], (DEPTH, D_MODEL, D_MODEL), D_MODEL ** -0.5),
        "w_ple_proj": nrm(ks[22], (DEPTH, PLE_DIM, D_MODEL), PLE_DIM ** -0.5),
    }


def reference(x_prompt, x_sample, cache_k, cache_v, state_conv, page_table, p_prompt, p_sample,
              norm_mix, w_in, q_norm, k_norm, att_bias, conv_w, attn_out_norm, conv_out_norm,
              w_out, norm_ffn, w_gate, w_up, w_down, ple_norm, w_ple_gate, w_ple_proj):
    bp, sp, _ = x_prompt.shape
    bs, ts, _ = x_sample.shape
    past = page_table.shape[1] * cache_k.shape[2]
    pos_p = jnp.arange(sp, dtype=jnp.int32)
    pos_q_s = past + jnp.arange(ts, dtype=jnp.int32)
    pos_k_s = jnp.arange(past + ts, dtype=jnp.int32)

    hp, hs = x_prompt, x_sample
    kp_list, vp_list, cp_list, ks_list, vs_list, cs_list = [], [], [], [], [], []
    for i in range(DEPTH):
        q, k, v, b_gate, c_gate, hc = in_projection(rmsnorm(hp, norm_mix[i]), w_in[i],
                                                    q_norm[i], k_norm[i])
        blocks = []
        for s0 in range(0, sp, Q_BLOCK):
            s1 = min(s0 + Q_BLOCK, sp)
            blocks.append(stick_breaking(q[:, s0:s1], k[:, :s1], v[:, :s1], att_bias[i],
                                         pos_p[s0:s1], pos_p[:s1]))
        attn_p = jnp.concatenate(blocks, axis=1)
        zero_buf = jnp.zeros((bp, CONV_WIDTH - 1, CONV_W), hc.dtype)
        conv_p, buf_p = short_conv(c_gate * hc, zero_buf, conv_w[i])
        hp = post_mixer(hp, attn_p, conv_p, b_gate, p_prompt[i], attn_out_norm[i],
                        conv_out_norm[i], w_out[i], norm_ffn[i], w_gate[i], w_up[i],
                        w_down[i], ple_norm[i], w_ple_gate[i], w_ple_proj[i])
        kp_list.append(k)
        vp_list.append(v)
        cp_list.append(buf_p)

        q2, k2, v2, b2, c2, hc2 = in_projection(rmsnorm(hs, norm_mix[i]), w_in[i],
                                                q_norm[i], k_norm[i])
        k_past = cache_k[i][page_table].reshape(bs, past, ATT_HEADS, HEAD_DIM)
        v_past = cache_v[i][page_table].reshape(bs, past, ATT_HEADS, HEAD_DIM)
        k_all = jnp.concatenate([k_past, k2.astype(k_past.dtype)], axis=1)
        v_all = jnp.concatenate([v_past, v2.astype(v_past.dtype)], axis=1)
        attn_s = stick_breaking(q2, k_all, v_all, att_bias[i], pos_q_s, pos_k_s)
        conv_s, buf_s = short_conv(c2 * hc2, state_conv[i], conv_w[i])
        hs = post_mixer(hs, attn_s, conv_s, b2, p_sample[i], attn_out_norm[i],
                        conv_out_norm[i], w_out[i], norm_ffn[i], w_gate[i], w_up[i],
                        w_down[i], ple_norm[i], w_ple_gate[i], w_ple_proj[i])
        ks_list.append(k2)
        vs_list.append(v2)
        cs_list.append(buf_s)

    k_prompt = jnp.stack(kp_list, axis=0)
    v_prompt = jnp.stack(vp_list, axis=0)
    conv_prompt = jnp.stack(cp_list, axis=0)
    k_sample = jnp.stack(ks_list, axis=0)
    v_sample = jnp.stack(vs_list, axis=0)
    conv_sample = jnp.stack(cs_list, axis=0)
    return (hp, hs, k_prompt, v_prompt, conv_prompt, k_sample, v_sample, conv_sample)
```

```python
import functools

import jax
import jax.numpy as jnp
from jax import lax
from jax.experimental import pallas as pl
from jax.experimental.pallas import tpu as pltpu

F32 = jnp.float32
BF16 = jnp.bfloat16

EPS = 1e-6
HEAD_DIM = 64

V7X_LANES = 128
V7X_SUBLANES = 8
V7X_MXU_DIM = 256
V7X_VMEM_BYTES = 64 * 1024 * 1024
VMEM_LIMIT_BYTES = V7X_VMEM_BYTES * 7 // 8

ROW_TILE_IN = 512
ROW_TILE_POST = 256
ATT_TQ = 256
ATT_TK = V7X_LANES
DEC_CHUNK_PAGES = 8


def _rms(x, g):
    ms = jnp.mean(x * x, axis=-1, keepdims=True)
    return x * lax.rsqrt(ms + EPS) * g


def _softplus(z):
    return jnp.maximum(z, 0.0) + jnp.log1p(jnp.exp(-jnp.abs(z)))


def _split_bf16(x):
    hi = x.astype(BF16)
    lo = (x - hi.astype(F32)).astype(BF16)
    return hi, lo


def _head_rms(t, gn, group_ones):
    sq = (t * t).astype(BF16)
    w = group_ones.shape[0]
    ss = jnp.concatenate(
        [jnp.dot(sq[:, i:i + w], group_ones, preferred_element_type=F32)
         for i in range(0, t.shape[1], w)], axis=1)
    return t * lax.rsqrt(ss * (1.0 / HEAD_DIM) + EPS) * gn


def _in_proj(x_ref, gmix_ref, win_ref, qn_ref, kn_ref, gones_ref):
    w = qn_ref.shape[1]
    hn = _rms(x_ref[...], gmix_ref[...]).astype(BF16)

    def sec(i):
        return jnp.dot(hn, win_ref[:, i * w:(i + 1) * w], preferred_element_type=F32)

    gones = gones_ref[...]
    q = _head_rms(sec(0), qn_ref[...], gones) * (HEAD_DIM ** -0.5)
    k = _head_rms(sec(1), kn_ref[...], gones)
    v = sec(2)
    b_gate = sec(3)
    u = sec(4) * sec(5)
    return q, k, v, b_gate, u


def _inproj_prompt_kernel(tiles_per_seq, x_ref, gmix_ref, win_ref, qn_ref, kn_ref, gones_ref,
                          cw_ref, cn_ref,
                          q_ref, k_ref, v_ref, kb_ref, vb_ref, cmix_ref, tail_ref, ubuf):
    tm = x_ref.shape[0]
    pad = V7X_SUBLANES
    q, k, v, b_gate, u = _in_proj(x_ref, gmix_ref, win_ref, qn_ref, kn_ref, gones_ref)
    q_ref[...] = q.astype(BF16)
    k_ref[...] = k
    v_ref[...] = v
    kb_ref[...] = k.astype(BF16)
    vb_ref[...] = v.astype(BF16)

    @pl.when(pl.program_id(0) % tiles_per_seq == 0)
    def _():
        ubuf[0:pad, :] = jnp.zeros((pad, ubuf.shape[1]), F32)

    ubuf[pad:pad + tm, :] = u
    cw = cw_ref[...]
    y = (cw[0:1] * ubuf[pad - 2:pad - 2 + tm, :] + cw[1:2] * ubuf[pad - 1:pad - 1 + tm, :]
         + cw[2:3] * u)
    cmix_ref[...] = _rms(b_gate * y, cn_ref[...]).astype(BF16)
    tail = u[tm - pad:, :]
    ubuf[0:pad, :] = tail
    tail_ref[0] = tail


def _inproj_sample_kernel(x_ref, gmix_ref, win_ref, qn_ref, kn_ref, gones_ref, cw_ref, cn_ref,
                          s0_ref, s1_ref,
                          q_ref, k_ref, v_ref, cmix_ref, u_ref):
    q, k, v, b_gate, u = _in_proj(x_ref, gmix_ref, win_ref, qn_ref, kn_ref, gones_ref)
    q_ref[...] = q
    k_ref[...] = k
    v_ref[...] = v
    u_ref[...] = u
    cw = cw_ref[...]
    y = cw[0:1] * s0_ref[...] + cw[1:2] * s1_ref[...] + cw[2:3] * u
    cmix_ref[...] = _rms(b_gate * y, cn_ref[...]).astype(BF16)


def _attn_prompt_kernel(bias_ref, q_ref, k_ref, v_ref, tt_ref, o_ref):
    hp = pl.program_id(1)
    qi = pl.program_id(2)
    tq, lanes = q_ref.shape
    tk = ATT_TK
    ratio = tq // tk
    q = q_ref[...]
    first_head = lax.broadcasted_iota(jnp.int32, (tq, lanes), 1) < HEAD_DIM
    zero_q = jnp.zeros_like(q)
    q_heads = (jnp.where(first_head, q, zero_q), jnp.where(first_head, zero_q, q))
    bias = (bias_ref[2 * hp], bias_ref[2 * hp + 1])
    tt = tt_ref[...]
    row = lax.broadcasted_iota(jnp.int32, (tq, tk), 0)
    col = lax.broadcasted_iota(jnp.int32, (tq, tk), 1)

    def block(j, carry, masked):
        acc, c0, c1 = carry
        start = pl.multiple_of(j * tk, tk)
        kb = k_ref[pl.ds(start, tk), :]
        vb = v_ref[pl.ds(start, tk), :]
        if masked:
            causal = (j * tk + col) < (qi * tq + row)
        later_sums = [c0, c1]
        for h in range(2):
            z = lax.dot_general(q_heads[h], kb, (((1,), (1,)), ((), ())),
                                preferred_element_type=F32) + bias[h]
            sp = _softplus(z)
            if masked:
                sp = jnp.where(causal, sp, 0.0)
            hi, lo = _split_bf16(sp)
            cum = jnp.dot(jnp.concatenate([hi, lo], axis=1), tt, preferred_element_type=F32)
            w = jnp.exp((z - sp) - (later_sums[h] + cum[:, :tk]))
            if masked:
                w = jnp.where(causal, w, 0.0)
            pv = jnp.dot(w.astype(BF16), vb, preferred_element_type=F32)
            keep = first_head if h == 0 else jnp.logical_not(first_head)
            acc = acc + jnp.where(keep, pv, 0.0)
            later_sums[h] = later_sums[h] + cum[:, tk:]
        return acc, later_sums[0], later_sums[1]

    zeros = jnp.zeros((tq, lanes), F32)
    carry = (zeros, zeros, zeros)
    for d in range(ratio - 1, -1, -1):
        carry = block(qi * ratio + d, carry, True)

    def pair(it, carry):
        j = qi * ratio - 1 - 2 * it
        carry = block(j, carry, False)
        return block(j - 1, carry, False)

    carry = lax.fori_loop(0, (qi * ratio) // 2, pair, carry)
    o_ref[...] = carry[0]


def _attn_decode_kernel(n_pages, pt_ref, q_ref, bias_ref, tri_ref, ck_hbm, cv_hbm, o_ref,
                        kbuf, vbuf, sem):
    b = pl.program_id(0)
    n_seq = pl.num_programs(0)
    page = ck_hbm.shape[1]
    n_heads = ck_hbm.shape[2]
    ch = DEC_CHUNK_PAGES
    n_chunks = n_pages // ch
    t = ch * page

    def chunk_copies(g, slot):
        seq = g // n_chunks
        c = g % n_chunks
        base = seq * n_pages + (n_chunks - 1 - c) * ch
        copies = []
        for p in range(ch):
            pid = pt_ref[base + p]
            rows = pl.ds(p * page, page)
            copies.append(pltpu.make_async_copy(ck_hbm.at[pid], kbuf.at[slot, rows], sem.at[0, slot]))
            copies.append(pltpu.make_async_copy(cv_hbm.at[pid], vbuf.at[slot, rows], sem.at[1, slot]))
        return copies

    @pl.when(b == 0)
    def _():
        for cp in chunk_copies(0, 0):
            cp.start()

    q = q_ref[0]
    sub_q = lax.broadcasted_iota(jnp.int32, q.shape, 0)
    sub_t = lax.broadcasted_iota(jnp.int32, (n_heads, t), 0)
    bias = bias_ref[...]
    tri = tri_ref[...]

    def chunk(c, state):
        later, acc = state
        g = b * n_chunks + c
        slot = g % 2

        @pl.when(g + 1 < n_seq * n_chunks)
        def _():
            for cp in chunk_copies(g + 1, 1 - slot):
                cp.start()

        for cp in chunk_copies(g, slot):
            cp.wait()

        z = jnp.zeros((n_heads, t), F32)
        for h in range(n_heads):
            kh = kbuf[slot, :, h, :]
            qh = jnp.where(sub_q == h, q, 0.0)
            z = z + lax.dot_general(qh, kh, (((1,), (1,)), ((), ())), preferred_element_type=F32)
        z = z + bias
        sp = _softplus(z)
        hi, lo = _split_bf16(sp)
        cum = jnp.dot(jnp.concatenate([hi, lo], axis=0), tri, preferred_element_type=F32)
        cum = cum[:n_heads] + cum[n_heads:]
        w = jnp.exp((z - sp) - (later + cum))
        for h in range(n_heads):
            vh = vbuf[slot, :, h, :]
            wh = jnp.where(sub_t == h, w, 0.0)
            acc = acc + jnp.dot(wh, vh, preferred_element_type=F32)
        later = later + jnp.sum(sp, axis=1, keepdims=True)
        return later, acc

    state = (jnp.zeros((n_heads, 1), F32), jnp.zeros((n_heads, HEAD_DIM), F32))
    _, acc = lax.fori_loop(0, n_chunks, chunk, state)
    o_ref[0] = acc


def _post_kernel(x_ref, attn_ref, cmix_ref, p_ref, gattn_ref, wout_ref, gffn_ref, wg_ref, wu_ref,
                 wd_ref, gple_ref, wpg_ref, wpp_ref, o_ref):
    att_w = attn_ref.shape[1]
    mix_a = _rms(attn_ref[...], gattn_ref[...]).astype(BF16)
    h = (x_ref[...]
         + jnp.dot(mix_a, wout_ref[0:att_w, :], preferred_element_type=F32)
         + jnp.dot(cmix_ref[...], wout_ref[att_w:, :], preferred_element_type=F32))
    hn = _rms(h, gffn_ref[...]).astype(BF16)
    g = jnp.dot(hn, wg_ref[...], preferred_element_type=F32)
    u = jnp.dot(hn, wu_ref[...], preferred_element_type=F32)
    act = (g * jax.nn.sigmoid(g) * u).astype(BF16)
    h = h + jnp.dot(act, wd_ref[...], preferred_element_type=F32)
    hp = _rms(h, gple_ref[...]).astype(BF16)
    gate = jax.nn.sigmoid(jnp.dot(hp, wpg_ref[...], preferred_element_type=F32))
    proj = jnp.dot(p_ref[...].astype(BF16), wpp_ref[...], preferred_element_type=F32)
    o_ref[...] = h + proj * gate


def _resident(shape):
    return pl.BlockSpec(shape, lambda *_: (0,) * len(shape), pipeline_mode=pl.Buffered(1))


def _row_spec(tm, width):
    return pl.BlockSpec((tm, width), lambda i: (i, 0))


def _params(*semantics):
    return pltpu.CompilerParams(dimension_semantics=semantics, vmem_limit_bytes=VMEM_LIMIT_BYTES)


def _inproj_weight_specs(d_model, w_total, att_w):
    return [_resident((1, d_model)), _resident((d_model, w_total)), _resident((1, att_w)),
            _resident((1, att_w)), _resident((V7X_MXU_DIM, V7X_MXU_DIM)),
            _resident((3, att_w)), _resident((1, att_w))]


def _inproj_prompt(x, seq_len, weights):
    m, d_model = x.shape
    w_total = weights[1].shape[1]
    att_w = weights[2].shape[1]
    tm = ROW_TILE_IN
    tiles_per_seq = seq_len // tm
    n_seq = m // seq_len
    rows = lambda dt: jax.ShapeDtypeStruct((m, att_w), dt)
    return pl.pallas_call(
        functools.partial(_inproj_prompt_kernel, tiles_per_seq),
        grid=(m // tm,),
        in_specs=[_row_spec(tm, d_model)] + _inproj_weight_specs(d_model, w_total, att_w),
        out_specs=[_row_spec(tm, att_w)] * 6
        + [pl.BlockSpec((1, V7X_SUBLANES, att_w), lambda i: (i // tiles_per_seq, 0, 0))],
        out_shape=[rows(BF16), rows(F32), rows(F32), rows(BF16), rows(BF16), rows(BF16),
                   jax.ShapeDtypeStruct((n_seq, V7X_SUBLANES, att_w), F32)],
        scratch_shapes=[pltpu.VMEM((tm + V7X_SUBLANES, att_w), F32)],
        compiler_params=_params("arbitrary"),
        name="inproj_prompt",
    )(x, *weights)


def _inproj_sample(x, s0, s1, weights):
    m, d_model = x.shape
    w_total = weights[1].shape[1]
    att_w = weights[2].shape[1]
    rows = lambda dt: jax.ShapeDtypeStruct((m, att_w), dt)
    return pl.pallas_call(
        _inproj_sample_kernel,
        grid=(1,),
        in_specs=[_row_spec(m, d_model)] + _inproj_weight_specs(d_model, w_total, att_w)
        + [_row_spec(m, att_w)] * 2,
        out_specs=[_row_spec(m, att_w)] * 5,
        out_shape=[rows(F32), rows(F32), rows(F32), rows(BF16), rows(F32)],
        compiler_params=_params("arbitrary"),
        name="inproj_sample",
    )(x, *weights, s0, s1)


def _cumsum_weights(tk):
    later = (jnp.arange(tk)[:, None] > jnp.arange(tk)[None, :]).astype(BF16)
    half = jnp.concatenate([later, jnp.ones((tk, tk), BF16)], axis=1)
    return jnp.concatenate([half, half], axis=0)


def _attn_prompt(q, k, v, att_bias, n_seq, seq_len):
    m, att_w = q.shape
    tq = ATT_TQ
    nq = seq_len // tq
    pair_w = 2 * HEAD_DIM
    assert pair_w == V7X_LANES and (tq // ATT_TK) % 2 == 0
    tt = _cumsum_weights(ATT_TK)
    return pl.pallas_call(
        _attn_prompt_kernel,
        grid=(n_seq, att_w // pair_w, nq),
        in_specs=[pl.BlockSpec(memory_space=pltpu.SMEM),
                  pl.BlockSpec((tq, pair_w), lambda b, hp, qi: (b * nq + qi, hp)),
                  pl.BlockSpec((seq_len, pair_w), lambda b, hp, qi: (b, hp)),
                  pl.BlockSpec((seq_len, pair_w), lambda b, hp, qi: (b, hp)),
                  pl.BlockSpec(tt.shape, lambda b, hp, qi: (0, 0))],
        out_specs=pl.BlockSpec((tq, pair_w), lambda b, hp, qi: (b * nq + qi, hp)),
        out_shape=jax.ShapeDtypeStruct((m, att_w), F32),
        compiler_params=_params("arbitrary", "arbitrary", "arbitrary"),
        name="attn_prompt",
    )(att_bias, q, k, v, tt)


def _attn_decode(q, att_bias, cache_k, cache_v, page_table):
    n_seq, n_heads, head_dim = q.shape
    n_pages = page_table.shape[1]
    page = cache_k.shape[1]
    assert n_pages % DEC_CHUNK_PAGES == 0
    t = DEC_CHUNK_PAGES * page
    tri = (jnp.arange(t)[:, None] > jnp.arange(t)[None, :]).astype(BF16)
    grid_spec = pltpu.PrefetchScalarGridSpec(
        num_scalar_prefetch=1,
        grid=(n_seq,),
        in_specs=[pl.BlockSpec((1, n_heads, head_dim), lambda b, pt: (b, 0, 0)),
                  pl.BlockSpec((n_heads, 1), lambda b, pt: (0, 0)),
                  pl.BlockSpec((t, t), lambda b, pt: (0, 0)),
                  pl.BlockSpec(memory_space=pl.ANY),
                  pl.BlockSpec(memory_space=pl.ANY)],
        out_specs=pl.BlockSpec((1, n_heads, head_dim), lambda b, pt: (b, 0, 0)),
        scratch_shapes=[pltpu.VMEM((2, t, n_heads, head_dim), F32),
                        pltpu.VMEM((2, t, n_heads, head_dim), F32),
                        pltpu.SemaphoreType.DMA((2, 2))],
    )
    return pl.pallas_call(
        functools.partial(_attn_decode_kernel, n_pages),
        grid_spec=grid_spec,
        out_shape=jax.ShapeDtypeStruct((n_seq, n_heads, head_dim), F32),
        compiler_params=_params("arbitrary"),
        name="attn_decode",
    )(page_table.reshape(-1), q, att_bias.reshape(n_heads, 1), tri, cache_k, cache_v)


def _post(x, attn, cmix, p, weights):
    m, d_model = x.shape
    tm = min(ROW_TILE_POST, m)
    specs = [_row_spec(tm, d_model), _row_spec(tm, attn.shape[1]), _row_spec(tm, cmix.shape[1]),
             _row_spec(tm, p.shape[1])] + [_resident(w.shape) for w in weights]
    return pl.pallas_call(
        _post_kernel,
        grid=(m // tm,),
        in_specs=specs,
        out_specs=_row_spec(tm, d_model),
        out_shape=jax.ShapeDtypeStruct((m, d_model), F32),
        compiler_params=_params("arbitrary"),
        name="post_mixer",
    )(x, attn, cmix, p, *weights)


def kernel(x_prompt, x_sample, cache_k, cache_v, state_conv, page_table, p_prompt, p_sample,
           norm_mix, w_in, q_norm, k_norm, att_bias, conv_w, attn_out_norm, conv_out_norm, w_out,
           norm_ffn, w_gate, w_up, w_down, ple_norm, w_ple_gate, w_ple_proj):
    bp, sp, d_model = x_prompt.shape
    bs, ts, _ = x_sample.shape
    depth = w_in.shape[0]
    att_w = w_in.shape[2] // 6
    n_heads = att_w // HEAD_DIM
    assert ts == 1 and conv_w.shape[1] == 3 and conv_w.shape[2] == att_w

    gidx = jnp.arange(V7X_MXU_DIM) // HEAD_DIM
    group_ones = (gidx[:, None] == gidx[None, :]).astype(BF16)
    row = lambda vec: vec.reshape(1, -1)

    hp = x_prompt.reshape(bp * sp, d_model)
    hs = x_sample.reshape(bs * ts, d_model)
    k_p, v_p, c_p, k_s, v_s, c_s = [], [], [], [], [], []
    for i in range(depth):
        in_w = (row(norm_mix[i]), w_in[i].astype(BF16), row(jnp.tile(q_norm[i], n_heads)),
                row(jnp.tile(k_norm[i], n_heads)), group_ones, conv_w[i], row(conv_out_norm[i]))
        post_w = (row(attn_out_norm[i]), w_out[i].astype(BF16), row(norm_ffn[i]),
                  w_gate[i].astype(BF16), w_up[i].astype(BF16), w_down[i].astype(BF16),
                  row(ple_norm[i]), w_ple_gate[i].astype(BF16), w_ple_proj[i].astype(BF16))

        q, k, v, kb, vb, cmix, tail = _inproj_prompt(hp, sp, in_w)
        attn = _attn_prompt(q, kb, vb, att_bias[i], bp, sp)
        hp = _post(hp, attn, cmix, p_prompt[i].reshape(bp * sp, -1), post_w)
        k_p.append(k.reshape(bp, sp, n_heads, HEAD_DIM))
        v_p.append(v.reshape(bp, sp, n_heads, HEAD_DIM))
        c_p.append(tail[:, V7X_SUBLANES - 2:, :])

        s0 = state_conv[i][:, 0, :]
        s1 = state_conv[i][:, 1, :]
        q2, k2, v2, cmix2, u2 = _inproj_sample(hs, s0, s1, in_w)
        attn2 = _attn_decode(q2.reshape(bs, n_heads, HEAD_DIM), att_bias[i], cache_k[i],
                             cache_v[i], page_table)
        hs = _post(hs, attn2.reshape(bs, att_w), cmix2, p_sample[i].reshape(bs * ts, -1), post_w)
        k_s.append(k2.reshape(bs, ts, n_heads, HEAD_DIM))
        v_s.append(v2.reshape(bs, ts, n_heads, HEAD_DIM))
        c_s.append(jnp.stack([s1, u2], axis=1))

    return (hp.reshape(bp, sp, d_model), hs.reshape(bs, ts, d_model),
            jnp.stack(k_p), jnp.stack(v_p), jnp.stack(c_p),
            jnp.stack(k_s), jnp.stack(v_s), jnp.stack(c_s))
```

```python
import functools

import jax
import jax.numpy as jnp
from jax import lax
from jax.experimental import pallas as pl
from jax.experimental.pallas import tpu as pltpu

F32 = jnp.float32
BF16 = jnp.bfloat16

EPS = 1e-6
HEAD_DIM = 64
LOG2E = 1.4426950408889634

V7X_LANES = 128
V7X_SUBLANES = 8
V7X_MXU_DIM = 256
V7X_VMEM_BYTES = 64 * 1024 * 1024
VMEM_LIMIT_BYTES = V7X_VMEM_BYTES * 7 // 8

ROW_TILE_IN = 512
ROW_TILE_POST = 256
ATT_TQ = 256
ATT_CK = 256
ATT_TK = V7X_LANES
DEC_CHUNK_PAGES = 16


def _rms(x, g):
    ms = jnp.mean(x * x, axis=-1, keepdims=True)
    return x * lax.rsqrt(ms + EPS) * g


def _softplus2(z2):
    return jnp.maximum(z2, 0.0) + jnp.log2(1.0 + jnp.exp2(-jnp.abs(z2)))


def _split_bf16(x):
    hi = x.astype(BF16)
    lo = (x - hi.astype(F32)).astype(BF16)
    return hi, lo


def _head_rms(t, gn, group_ones):
    sq = (t * t).astype(BF16)
    w = group_ones.shape[0]
    ss = jnp.concatenate(
        [jnp.dot(sq[:, i:i + w], group_ones, preferred_element_type=F32)
         for i in range(0, t.shape[1], w)], axis=1)
    return t * lax.rsqrt(ss * (1.0 / HEAD_DIM) + EPS) * gn


def _in_proj(x_ref, gmix_ref, win_ref, qn_ref, kn_ref, gones_ref):
    w = qn_ref.shape[1]
    hn = _rms(x_ref[...], gmix_ref[...]).astype(BF16)

    def sec(i):
        return jnp.dot(hn, win_ref[:, i * w:(i + 1) * w], preferred_element_type=F32)

    gones = gones_ref[...]
    q = _head_rms(sec(0), qn_ref[...], gones) * (HEAD_DIM ** -0.5 * LOG2E)
    k = _head_rms(sec(1), kn_ref[...], gones)
    v = sec(2)
    b_gate = sec(3)
    u = sec(4) * sec(5)
    return q, k, v, b_gate, u


def _inproj_prompt_kernel(tiles_per_seq, x_ref, gmix_ref, win_ref, qn_ref, kn_ref, gones_ref,
                          cw_ref, cn_ref,
                          q_ref, k_ref, v_ref, kb_ref, vb_ref, cmix_ref, tail_ref, ubuf):
    tm = x_ref.shape[0]
    pad = V7X_SUBLANES
    q, k, v, b_gate, u = _in_proj(x_ref, gmix_ref, win_ref, qn_ref, kn_ref, gones_ref)
    q_ref[...] = q.astype(BF16)
    k_ref[...] = k
    v_ref[...] = v
    kb_ref[...] = k.astype(BF16)
    vb_ref[...] = v.astype(BF16)

    @pl.when(pl.program_id(0) % tiles_per_seq == 0)
    def _():
        ubuf[0:pad, :] = jnp.zeros((pad, ubuf.shape[1]), F32)

    ubuf[pad:pad + tm, :] = u
    cw = cw_ref[...]
    y = (cw[0:1] * ubuf[pad - 2:pad - 2 + tm, :] + cw[1:2] * ubuf[pad - 1:pad - 1 + tm, :]
         + cw[2:3] * u)
    cmix_ref[...] = _rms(b_gate * y, cn_ref[...]).astype(BF16)
    tail = u[tm - pad:, :]
    ubuf[0:pad, :] = tail
    tail_ref[0] = tail


def _inproj_sample_kernel(x_ref, gmix_ref, win_ref, qn_ref, kn_ref, gones_ref, cw_ref, cn_ref,
                          s0_ref, s1_ref,
                          q_ref, k_ref, v_ref, cmix_ref, u_ref):
    q, k, v, b_gate, u = _in_proj(x_ref, gmix_ref, win_ref, qn_ref, kn_ref, gones_ref)
    q_ref[...] = q
    k_ref[...] = k
    v_ref[...] = v
    u_ref[...] = u
    cw = cw_ref[...]
    y = cw[0:1] * s0_ref[...] + cw[1:2] * s1_ref[...] + cw[2:3] * u
    cmix_ref[...] = _rms(b_gate * y, cn_ref[...]).astype(BF16)


def _attn_prompt_kernel(bias_ref, q_ref, k_ref, v_ref, tt_ref, o_ref,
                        zbuf, lbbuf, hlbuf, wbuf, laterbuf):
    hp = pl.program_id(1)
    seq_len, lanes = q_ref.shape
    tq, ck, tk = ATT_TQ, ATT_CK, ATT_TK
    nq = seq_len // tq
    nkb = ck // tk
    nt = (((1,), (1,)), ((), ()))
    first_head = lax.broadcasted_iota(jnp.int32, (tq, lanes), 1) < HEAD_DIM
    bias = (bias_ref[2 * hp] * LOG2E, bias_ref[2 * hp + 1] * LOG2E)
    tt = tt_ref[...]
    causal = (lax.broadcasted_iota(jnp.int32, (tq, ck), 1)
              < lax.broadcasted_iota(jnp.int32, (tq, ck), 0))

    @pl.when(jnp.logical_and(pl.program_id(0) == 0, hp == 0))
    def _():
        zbuf[...] = jnp.zeros(zbuf.shape, zbuf.dtype)
        lbbuf[...] = jnp.zeros(lbbuf.shape, lbbuf.dtype)
        hlbuf[...] = jnp.zeros(hlbuf.shape, hlbuf.dtype)
        wbuf[...] = jnp.zeros(wbuf.shape, wbuf.dtype)
        laterbuf[...] = jnp.zeros(laterbuf.shape, laterbuf.dtype)

    o_ref[...] = jnp.zeros(o_ref.shape, o_ref.dtype)

    def rows_of(qi):
        return pl.ds(pl.multiple_of(qi * tq, tq), tq)

    def keys_of(kj):
        return pl.ds(pl.multiple_of(kj * ck, ck), ck)

    def stage_qk(slot, qi, kj):
        q = q_ref[rows_of(qi), :]
        k = k_ref[keys_of(kj), :]
        zero = jnp.zeros_like(q)
        for h in range(2):
            qh = jnp.where(first_head, q, zero) if h == 0 else jnp.where(first_head, zero, q)
            zbuf[slot, h] = lax.dot_general(qh, k, nt, preferred_element_type=F32) + bias[h]

    def stage_softplus(slot, masked):
        for h in range(2):
            z = zbuf[slot, h]
            sp = _softplus2(z)
            if masked:
                sp = jnp.where(causal, sp, 0.0)
            lbbuf[slot, h] = z - sp
            for kb in range(nkb):
                hi, lo = _split_bf16(sp[:, kb * tk:(kb + 1) * tk])
                hlbuf[slot, h, kb] = jnp.concatenate([hi, lo], axis=1)

    def stage_weights(slot, qi, valid, masked):
        rows = rows_of(qi)
        for h in range(2):
            old = laterbuf[h, rows, :]
            lat = jnp.zeros_like(old) if masked else old
            for kb in range(nkb - 1, -1, -1):
                cum = jnp.dot(hlbuf[slot, h, kb], tt, preferred_element_type=F32)
                cols = slice(kb * tk, (kb + 1) * tk)
                w = jnp.exp2(lbbuf[slot, h, :, cols] - (lat + cum[:, :tk]))
                if masked:
                    w = jnp.where(causal[:, cols], w, 0.0)
                wbuf[slot, h, :, cols] = w.astype(BF16)
                lat = lat + cum[:, tk:]
            laterbuf[h, rows, :] = jnp.where(valid, lat, old)

    def stage_pv(slot, qi, kj, valid, masked):
        rows = rows_of(qi)
        v = v_ref[keys_of(kj), :]
        pv = [jnp.dot(wbuf[slot, h], v, preferred_element_type=F32) for h in range(2)]
        acc = jnp.where(valid, jnp.where(first_head, pv[0], pv[1]), 0.0)
        o_ref[rows, :] = o_ref[rows, :] + acc

    def run_pass(masked, n_items, first, advance):
        def step(i, parity, coords):
            c1, c2, c3, c4 = coords

            def valid(s):
                return jnp.logical_and(i - s >= 0, i - s < n_items)

            stage_pv((parity + 3) % 2, c4[0], c4[1], valid(3), masked)
            stage_weights((parity + 2) % 2, c3[0], valid(2), masked)
            stage_softplus((parity + 1) % 2, masked)
            stage_qk(parity, c1[0], c1[1])
            nxt = advance(*c1)
            more = i + 1 < n_items
            c1n = (jnp.where(more, nxt[0], c1[0]), jnp.where(more, nxt[1], c1[1]))
            return c1n, c1, c2, c3

        def two_steps(i2, coords):
            return step(2 * i2 + 1, 1, step(2 * i2, 0, coords))

        lax.fori_loop(0, (n_items + 3 + 1) // 2, two_steps, (first,) * 4)

    def next_diagonal(qi, kj):
        return qi + 1, kj + 1

    def next_below(qi, kj):
        wrap = kj == 0
        return jnp.where(wrap, qi + 1, qi), jnp.where(wrap, qi, kj - 1)

    run_pass(True, nq, (jnp.int32(0), jnp.int32(0)), next_diagonal)
    run_pass(False, nq * (nq - 1) // 2, (jnp.int32(1), jnp.int32(0)), next_below)


def _attn_decode_kernel(n_pages, pt_ref, q_ref, bias_ref, tt_ref, ck_hbm, cv_hbm, o_ref,
                        kbuf, vbuf, acc_ref, sem):
    b = pl.program_id(0)
    n_seq = pl.num_programs(0)
    _, n_heads, hd, page = ck_hbm.shape
    ch = DEC_CHUNK_PAGES
    n_chunks = n_pages // ch

    def chunk_copies(g, slot):
        seq = g // n_chunks
        c = g % n_chunks
        base = seq * n_pages + (n_chunks - 1 - c) * ch
        copies = []
        for p in range(ch):
            pid = pt_ref[base + p]
            copies.append(pltpu.make_async_copy(ck_hbm.at[pid], kbuf.at[slot, p], sem.at[0, slot]))
            copies.append(pltpu.make_async_copy(cv_hbm.at[pid], vbuf.at[slot, p], sem.at[1, slot]))
        return copies

    @pl.when(b == 0)
    def _():
        for cp in chunk_copies(0, 0):
            cp.start()

    qcol = jnp.broadcast_to(q_ref[0], (page, n_heads * hd)).T
    acc_ref[...] = jnp.zeros(acc_ref.shape, F32)
    sub = lax.broadcasted_iota(jnp.int32, (n_heads, page), 0)
    bias = bias_ref[...] * LOG2E
    tt = tt_ref[...]

    def chunk(c, later):
        g = b * n_chunks + c
        slot = g % 2

        @pl.when(g + 1 < n_seq * n_chunks)
        def _():
            for cp in chunk_copies(g + 1, 1 - slot):
                cp.start()

        for cp in chunk_copies(g, slot):
            cp.wait()

        zs = []
        for p in range(ch):
            zp = jnp.zeros((n_heads, page), F32)
            for h in range(n_heads):
                prod = kbuf[slot, p, h] * qcol[h * hd:(h + 1) * hd]
                zp = jnp.where(sub == h, jnp.sum(prod, axis=0, keepdims=True), zp)
            zs.append(zp)
        z = jnp.concatenate(zs, axis=0) + bias
        sp = _softplus2(z)
        hi, lo = _split_bf16(sp)
        cum = jnp.dot(jnp.concatenate([hi, lo], axis=1), tt, preferred_element_type=F32)
        laters = [None] * ch
        for p in range(ch - 1, -1, -1):
            laters[p] = later
            later = later + cum[p * n_heads:(p + 1) * n_heads, page:]
        w = jnp.exp2((z - sp) - (jnp.concatenate(laters, axis=0) + cum[:, :page]))
        for h in range(n_heads):
            rows = slice(h * hd, (h + 1) * hd)
            a = acc_ref[rows, :]
            for p in range(ch):
                r = p * n_heads + h
                a = a + w[r:r + 1, :] * vbuf[slot, p, h]
            acc_ref[rows, :] = a
        return later

    lax.fori_loop(0, n_chunks, chunk, jnp.zeros((n_heads, page), F32))
    hi, lo = _split_bf16(acc_ref[...])
    ones = jnp.ones((V7X_SUBLANES, page), BF16)
    nt = (((1,), (1,)), ((), ()))
    out = (lax.dot_general(ones, hi, nt, preferred_element_type=F32)
           + lax.dot_general(ones, lo, nt, preferred_element_type=F32))
    o_ref[0] = out[0:1]


def _post_kernel(x_ref, attn_ref, cmix_ref, p_ref, gattn_ref, wout_ref, gffn_ref, wg_ref, wu_ref,
                 wd_ref, gple_ref, wpg_ref, wpp_ref, o_ref):
    att_w = attn_ref.shape[1]
    mix_a = _rms(attn_ref[...], gattn_ref[...]).astype(BF16)
    h = (x_ref[...]
         + jnp.dot(mix_a, wout_ref[0:att_w, :], preferred_element_type=F32)
         + jnp.dot(cmix_ref[...], wout_ref[att_w:, :], preferred_element_type=F32))
    hn = _rms(h, gffn_ref[...]).astype(BF16)
    g = jnp.dot(hn, wg_ref[...], preferred_element_type=F32)
    u = jnp.dot(hn, wu_ref[...], preferred_element_type=F32)
    act = (g * jax.nn.sigmoid(g) * u).astype(BF16)
    h = h + jnp.dot(act, wd_ref[...], preferred_element_type=F32)
    hp = _rms(h, gple_ref[...]).astype(BF16)
    gate = jax.nn.sigmoid(jnp.dot(hp, wpg_ref[...], preferred_element_type=F32))
    proj = jnp.dot(p_ref[...].astype(BF16), wpp_ref[...], preferred_element_type=F32)
    o_ref[...] = h + proj * gate


def _resident(shape):
    return pl.BlockSpec(shape, lambda *_: (0,) * len(shape), pipeline_mode=pl.Buffered(1))


def _row_spec(tm, width):
    return pl.BlockSpec((tm, width), lambda i: (i, 0))


def _params(*semantics):
    return pltpu.CompilerParams(dimension_semantics=semantics, vmem_limit_bytes=VMEM_LIMIT_BYTES)


def _inproj_weight_specs(d_model, w_total, att_w):
    return [_resident((1, d_model)), _resident((d_model, w_total)), _resident((1, att_w)),
            _resident((1, att_w)), _resident((V7X_MXU_DIM, V7X_MXU_DIM)),
            _resident((3, att_w)), _resident((1, att_w))]


def _inproj_prompt(x, seq_len, weights):
    m, d_model = x.shape
    w_total = weights[1].shape[1]
    att_w = weights[2].shape[1]
    tm = ROW_TILE_IN
    tiles_per_seq = seq_len // tm
    n_seq = m // seq_len
    rows = lambda dt: jax.ShapeDtypeStruct((m, att_w), dt)
    return pl.pallas_call(
        functools.partial(_inproj_prompt_kernel, tiles_per_seq),
        grid=(m // tm,),
        in_specs=[_row_spec(tm, d_model)] + _inproj_weight_specs(d_model, w_total, att_w),
        out_specs=[_row_spec(tm, att_w)] * 6
        + [pl.BlockSpec((1, V7X_SUBLANES, att_w), lambda i: (i // tiles_per_seq, 0, 0))],
        out_shape=[rows(BF16), rows(F32), rows(F32), rows(BF16), rows(BF16), rows(BF16),
                   jax.ShapeDtypeStruct((n_seq, V7X_SUBLANES, att_w), F32)],
        scratch_shapes=[pltpu.VMEM((tm + V7X_SUBLANES, att_w), F32)],
        compiler_params=_params("arbitrary"),
        name="inproj_prompt",
    )(x, *weights)


def _inproj_sample(x, s0, s1, weights):
    m, d_model = x.shape
    w_total = weights[1].shape[1]
    att_w = weights[2].shape[1]
    rows = lambda dt: jax.ShapeDtypeStruct((m, att_w), dt)
    return pl.pallas_call(
        _inproj_sample_kernel,
        grid=(1,),
        in_specs=[_row_spec(m, d_model)] + _inproj_weight_specs(d_model, w_total, att_w)
        + [_row_spec(m, att_w)] * 2,
        out_specs=[_row_spec(m, att_w)] * 5,
        out_shape=[rows(F32), rows(F32), rows(F32), rows(BF16), rows(F32)],
        compiler_params=_params("arbitrary"),
        name="inproj_sample",
    )(x, *weights, s0, s1)


def _cumsum_weights(tk):
    later = (jnp.arange(tk)[:, None] > jnp.arange(tk)[None, :]).astype(BF16)
    half = jnp.concatenate([later, jnp.ones((tk, tk), BF16)], axis=1)
    return jnp.concatenate([half, half], axis=0)


def _attn_prompt(q, k, v, att_bias, n_seq, seq_len):
    m, att_w = q.shape
    tq, ck, tk = ATT_TQ, ATT_CK, ATT_TK
    pair_w = 2 * HEAD_DIM
    assert pair_w == V7X_LANES and tq == ck and ck % tk == 0 and seq_len % tq == 0
    tt = _cumsum_weights(tk)
    seq_block = pl.BlockSpec((seq_len, pair_w), lambda b, hp: (b, hp))
    return pl.pallas_call(
        _attn_prompt_kernel,
        grid=(n_seq, att_w // pair_w),
        in_specs=[pl.BlockSpec(memory_space=pltpu.SMEM), seq_block, seq_block, seq_block,
                  pl.BlockSpec(tt.shape, lambda b, hp: (0, 0))],
        out_specs=seq_block,
        out_shape=jax.ShapeDtypeStruct((m, att_w), F32),
        scratch_shapes=[pltpu.VMEM((2, 2, tq, ck), F32),
                        pltpu.VMEM((2, 2, tq, ck), F32),
                        pltpu.VMEM((2, 2, ck // tk, tq, 2 * tk), BF16),
                        pltpu.VMEM((2, 2, tq, ck), BF16),
                        pltpu.VMEM((2, seq_len, pair_w), F32)],
        compiler_params=_params("arbitrary", "arbitrary"),
        name="attn_prompt",
    )(att_bias, q, k, v, tt)


def _attn_decode(q, att_bias, cache_k, cache_v, page_table):
    n_seq, att_w = q.shape
    n_pages = page_table.shape[1]
    _, page, n_heads, head_dim = cache_k.shape
    ch = DEC_CHUNK_PAGES
    assert n_pages % ch == 0 and page == V7X_LANES and n_heads == V7X_SUBLANES
    ck = jnp.transpose(cache_k, (0, 2, 3, 1))
    cv = jnp.transpose(cache_v, (0, 2, 3, 1))
    bias = jnp.tile(att_bias.reshape(n_heads, 1), (ch, page))
    tt = _cumsum_weights(page)
    buf = pltpu.VMEM((2, ch, n_heads, head_dim, page), F32)
    grid_spec = pltpu.PrefetchScalarGridSpec(
        num_scalar_prefetch=1,
        grid=(n_seq,),
        in_specs=[pl.BlockSpec((1, 1, att_w), lambda b, pt: (b, 0, 0)),
                  pl.BlockSpec(bias.shape, lambda b, pt: (0, 0)),
                  pl.BlockSpec(tt.shape, lambda b, pt: (0, 0)),
                  pl.BlockSpec(memory_space=pl.ANY),
                  pl.BlockSpec(memory_space=pl.ANY)],
        out_specs=pl.BlockSpec((1, 1, att_w), lambda b, pt: (b, 0, 0)),
        scratch_shapes=[buf, buf, pltpu.VMEM((att_w, page), F32),
                        pltpu.SemaphoreType.DMA((2, 2))],
    )
    out = pl.pallas_call(
        functools.partial(_attn_decode_kernel, n_pages),
        grid_spec=grid_spec,
        out_shape=jax.ShapeDtypeStruct((n_seq, 1, att_w), F32),
        compiler_params=_params("arbitrary"),
        name="attn_decode",
    )(page_table.reshape(-1), q.reshape(n_seq, 1, att_w), bias, tt, ck, cv)
    return out.reshape(n_seq, att_w)


def _post(x, attn, cmix, p, weights):
    m, d_model = x.shape
    tm = min(ROW_TILE_POST, m)
    specs = [_row_spec(tm, d_model), _row_spec(tm, attn.shape[1]), _row_spec(tm, cmix.shape[1]),
             _row_spec(tm, p.shape[1])] + [_resident(w.shape) for w in weights]
    return pl.pallas_call(
        _post_kernel,
        grid=(m // tm,),
        in_specs=specs,
        out_specs=_row_spec(tm, d_model),
        out_shape=jax.ShapeDtypeStruct((m, d_model), F32),
        compiler_params=_params("arbitrary"),
        name="post_mixer",
    )(x, attn, cmix, p, *weights)


def kernel(x_prompt, x_sample, cache_k, cache_v, state_conv, page_table, p_prompt, p_sample,
           norm_mix, w_in, q_norm, k_norm, att_bias, conv_w, attn_out_norm, conv_out_norm, w_out,
           norm_ffn, w_gate, w_up, w_down, ple_norm, w_ple_gate, w_ple_proj):
    bp, sp, d_model = x_prompt.shape
    bs, ts, _ = x_sample.shape
    depth = w_in.shape[0]
    att_w = w_in.shape[2] // 6
    n_heads = att_w // HEAD_DIM
    assert ts == 1 and conv_w.shape[1] == 3 and conv_w.shape[2] == att_w

    gidx = jnp.arange(V7X_MXU_DIM) // HEAD_DIM
    group_ones = (gidx[:, None] == gidx[None, :]).astype(BF16)
    row = lambda vec: vec.reshape(1, -1)

    hp = x_prompt.reshape(bp * sp, d_model)
    hs = x_sample.reshape(bs * ts, d_model)
    k_p, v_p, c_p, k_s, v_s, c_s = [], [], [], [], [], []
    for i in range(depth):
        in_w = (row(norm_mix[i]), w_in[i].astype(BF16), row(jnp.tile(q_norm[i], n_heads)),
                row(jnp.tile(k_norm[i], n_heads)), group_ones, conv_w[i], row(conv_out_norm[i]))
        post_w = (row(attn_out_norm[i]), w_out[i].astype(BF16), row(norm_ffn[i]),
                  w_gate[i].astype(BF16), w_up[i].astype(BF16), w_down[i].astype(BF16),
                  row(ple_norm[i]), w_ple_gate[i].astype(BF16), w_ple_proj[i].astype(BF16))

        q, k, v, kb, vb, cmix, tail = _inproj_prompt(hp, sp, in_w)
        attn = _attn_prompt(q, kb, vb, att_bias[i], bp, sp)
        hp = _post(hp, attn, cmix, p_prompt[i].reshape(bp * sp, -1), post_w)
        k_p.append(k.reshape(bp, sp, n_heads, HEAD_DIM))
        v_p.append(v.reshape(bp, sp, n_heads, HEAD_DIM))
        c_p.append(tail[:, V7X_SUBLANES - 2:, :])

        s0 = state_conv[i][:, 0, :]
        s1 = state_conv[i][:, 1, :]
        q2, k2, v2, cmix2, u2 = _inproj_sample(hs, s0, s1, in_w)
        attn2 = _attn_decode(q2, att_bias[i], cache_k[i], cache_v[i], page_table)
        hs = _post(hs, attn2, cmix2, p_sample[i].reshape(bs * ts, -1), post_w)
        k_s.append(k2.reshape(bs, ts, n_heads, HEAD_DIM))
        v_s.append(v2.reshape(bs, ts, n_heads, HEAD_DIM))
        c_s.append(jnp.stack([s1, u2], axis=1))

    return (hp.reshape(bp, sp, d_model), hs.reshape(bs, ts, d_model),
            jnp.stack(k_p), jnp.stack(v_p), jnp.stack(c_p),
            jnp.stack(k_s), jnp.stack(v_s), jnp.stack(c_s))
```

```python
import functools

import jax
import jax.numpy as jnp
from jax import lax
from jax.experimental import pallas as pl
from jax.experimental.pallas import tpu as pltpu

F32 = jnp.float32
BF16 = jnp.bfloat16

EPS = 1e-6
HEAD_DIM = 64
LOG2E = 1.4426950408889634

V7X_LANES = 128
V7X_SUBLANES = 8
V7X_MXU_DIM = 256
V7X_VMEM_BYTES = 64 * 1024 * 1024
VMEM_LIMIT_BYTES = V7X_VMEM_BYTES * 7 // 8

ROW_TILE_IN = 512
ROW_TILE_POST = 256
ATT_TQ = 256
ATT_CK = 256
ATT_TK = V7X_LANES
DEC_CHUNK_PAGES = 8


def _rms(x, g):
    ms = jnp.mean(x * x, axis=-1, keepdims=True)
    return x * lax.rsqrt(ms + EPS) * g


def _softplus2(z2):
    return jnp.maximum(z2, 0.0) + jnp.log2(1.0 + jnp.exp2(-jnp.abs(z2)))


def _split_bf16(x):
    hi = x.astype(BF16)
    lo = (x - hi.astype(F32)).astype(BF16)
    return hi, lo


def _head_rms(t, gn, group_ones):
    sq = (t * t).astype(BF16)
    w = group_ones.shape[0]
    ss = jnp.concatenate(
        [jnp.dot(sq[:, i:i + w], group_ones, preferred_element_type=F32)
         for i in range(0, t.shape[1], w)], axis=1)
    return t * lax.rsqrt(ss * (1.0 / HEAD_DIM) + EPS) * gn


def _in_proj(x_ref, gmix_ref, win_ref, qn_ref, kn_ref, gones_ref):
    w = qn_ref.shape[1]
    hn = _rms(x_ref[...], gmix_ref[...]).astype(BF16)

    def sec(i):
        return jnp.dot(hn, win_ref[:, i * w:(i + 1) * w], preferred_element_type=F32)

    gones = gones_ref[...]
    q = _head_rms(sec(0), qn_ref[...], gones) * (HEAD_DIM ** -0.5 * LOG2E)
    k = _head_rms(sec(1), kn_ref[...], gones)
    v = sec(2)
    b_gate = sec(3)
    u = sec(4) * sec(5)
    return q, k, v, b_gate, u


def _inproj_prompt_kernel(tiles_per_seq, x_ref, gmix_ref, win_ref, qn_ref, kn_ref, gones_ref,
                          cw_ref, cn_ref,
                          q_ref, kt_ref, vt_ref, kb_ref, vb_ref, cmix_ref, tail_ref, ubuf):
    tm = x_ref.shape[0]
    pad = V7X_SUBLANES
    q, k, v, b_gate, u = _in_proj(x_ref, gmix_ref, win_ref, qn_ref, kn_ref, gones_ref)
    q_ref[...] = q.astype(BF16)
    kt_ref[0] = k.T
    vt_ref[0] = v.T
    kb_ref[...] = k.astype(BF16)
    vb_ref[...] = v.astype(BF16)

    @pl.when(pl.program_id(0) % tiles_per_seq == 0)
    def _():
        ubuf[0:pad, :] = jnp.zeros((pad, ubuf.shape[1]), F32)

    ubuf[pad:pad + tm, :] = u
    cw = cw_ref[...]
    y = (cw[0:1] * ubuf[pad - 2:pad - 2 + tm, :] + cw[1:2] * ubuf[pad - 1:pad - 1 + tm, :]
         + cw[2:3] * u)
    cmix_ref[...] = _rms(b_gate * y, cn_ref[...]).astype(BF16)
    tail = u[tm - pad:, :]
    ubuf[0:pad, :] = tail
    tail_ref[0] = tail


def _inproj_sample_kernel(x_ref, gmix_ref, win_ref, qn_ref, kn_ref, gones_ref, cw_ref, cn_ref,
                          s0_ref, s1_ref,
                          q_ref, k_ref, v_ref, cmix_ref, u_ref):
    q, k, v, b_gate, u = _in_proj(x_ref, gmix_ref, win_ref, qn_ref, kn_ref, gones_ref)
    q_ref[...] = q
    k_ref[...] = k
    v_ref[...] = v
    u_ref[...] = u
    cw = cw_ref[...]
    y = cw[0:1] * s0_ref[...] + cw[1:2] * s1_ref[...] + cw[2:3] * u
    cmix_ref[...] = _rms(b_gate * y, cn_ref[...]).astype(BF16)


def _attention_kernel(n_pages, seqs_per_step, pt_ref, bias_ref, q_ref, k_ref, v_ref, tt_ref, q2_ref, dbias_ref,
                      ck_hbm, cv_hbm, o_ref, o2_ref,
                      zbuf, lbbuf, hlbuf, wbuf, laterbuf, kbuf, vbuf, dacc, dlater, qcol, sem):
    hp = pl.program_id(1)
    seq_len, lanes = q_ref.shape
    tq, ck, tk = ATT_TQ, ATT_CK, ATT_TK
    nq = seq_len // tq
    nkb = ck // tk
    nt = (((1,), (1,)), ((), ()))
    first_head = lax.broadcasted_iota(jnp.int32, (tq, lanes), 1) < HEAD_DIM
    bias = (bias_ref[2 * hp] * LOG2E, bias_ref[2 * hp + 1] * LOG2E)
    tt = tt_ref[...]
    causal = (lax.broadcasted_iota(jnp.int32, (tq, ck), 1)
              < lax.broadcasted_iota(jnp.int32, (tq, ck), 0))

    @pl.when(jnp.logical_and(pl.program_id(0) == 0, hp == 0))
    def _():
        zbuf[...] = jnp.zeros(zbuf.shape, zbuf.dtype)
        lbbuf[...] = jnp.zeros(lbbuf.shape, lbbuf.dtype)
        hlbuf[...] = jnp.zeros(hlbuf.shape, hlbuf.dtype)
        wbuf[...] = jnp.zeros(wbuf.shape, wbuf.dtype)
        laterbuf[...] = jnp.zeros(laterbuf.shape, laterbuf.dtype)

    o_ref[...] = jnp.zeros(o_ref.shape, o_ref.dtype)

    step_id = pl.program_id(0) * pl.num_programs(1) + hp
    _, n_heads, hd, page = ck_hbm.shape
    ch = kbuf.shape[1]
    n_chunks = n_pages // ch
    chunks_per_step = seqs_per_step * n_chunks
    n_total_chunks = o2_ref.shape[0] * n_chunks
    sub = lax.broadcasted_iota(jnp.int32, (n_heads, page), 0)
    ones_row = jnp.ones((V7X_SUBLANES, page), BF16)

    def chunk_copies(g, slot):
        seq = g // n_chunks
        c = g % n_chunks
        base = seq * n_pages + (n_chunks - 1 - c) * ch
        copies = []
        for p in range(ch):
            pid = pt_ref[base + p]
            copies.append(pltpu.make_async_copy(ck_hbm.at[pid], kbuf.at[slot, p], sem.at[0, slot]))
            copies.append(pltpu.make_async_copy(cv_hbm.at[pid], vbuf.at[slot, p], sem.at[1, slot]))
        return copies

    @pl.when(step_id == 0)
    def _():
        for cp in chunk_copies(0, 0):
            cp.start()

    def decode_chunk(g):
        seq = g // n_chunks
        c = g % n_chunks
        slot = g % 2

        @pl.when(g + 1 < n_total_chunks)
        def _():
            for cp in chunk_copies(g + 1, 1 - slot):
                cp.start()

        for cp in chunk_copies(g, slot):
            cp.wait()

        @pl.when(c == 0)
        def _():
            qcol[...] = jnp.broadcast_to(q2_ref[seq], (page, n_heads * hd)).T
            dacc[...] = jnp.zeros(dacc.shape, F32)
            dlater[...] = jnp.zeros(dlater.shape, F32)

        zs = []
        for p in range(ch):
            zp = jnp.zeros((n_heads, page), F32)
            for h in range(n_heads):
                prod = kbuf[slot, p, h] * qcol[h * hd:(h + 1) * hd, :]
                zp = jnp.where(sub == h, jnp.sum(prod, axis=0, keepdims=True), zp)
            zs.append(zp)
        z = jnp.concatenate(zs, axis=0) + dbias_ref[...] * LOG2E
        sp = _softplus2(z)
        hi, lo = _split_bf16(sp)
        cum = jnp.dot(jnp.concatenate([hi, lo], axis=1), tt, preferred_element_type=F32)
        later = dlater[...]
        laters = [None] * ch
        for p in range(ch - 1, -1, -1):
            laters[p] = later
            later = later + cum[p * n_heads:(p + 1) * n_heads, page:]
        dlater[...] = later
        w = jnp.exp2((z - sp) - (jnp.concatenate(laters, axis=0) + cum[:, :page]))
        for h in range(n_heads):
            rows = slice(h * hd, (h + 1) * hd)
            a = dacc[rows, :]
            for p in range(ch):
                r = p * n_heads + h
                a = a + w[r:r + 1, :] * vbuf[slot, p, h]
            dacc[rows, :] = a

        @pl.when(c == n_chunks - 1)
        def _():
            hi_a, lo_a = _split_bf16(dacc[...])
            out = (lax.dot_general(ones_row, hi_a, nt, preferred_element_type=F32)
                   + lax.dot_general(ones_row, lo_a, nt, preferred_element_type=F32))
            o2_ref[seq] = out[0:1]

    def decode_trip(trip):
        @pl.when(trip < chunks_per_step)
        def _():
            decode_chunk(step_id * chunks_per_step + trip)

    def rows_of(qi):
        return pl.ds(pl.multiple_of(qi * tq, tq), tq)

    def keys_of(kj):
        return pl.ds(pl.multiple_of(kj * ck, ck), ck)

    def stage_qk(slot, qi, kj):
        q = q_ref[rows_of(qi), :]
        k = k_ref[keys_of(kj), :]
        zero = jnp.zeros_like(q)
        for h in range(2):
            qh = jnp.where(first_head, q, zero) if h == 0 else jnp.where(first_head, zero, q)
            zbuf[slot, h] = lax.dot_general(qh, k, nt, preferred_element_type=F32) + bias[h]

    def stage_softplus(slot, masked):
        for h in range(2):
            z = zbuf[slot, h]
            sp = _softplus2(z)
            if masked:
                sp = jnp.where(causal, sp, 0.0)
            lbbuf[slot, h] = z - sp
            for kb in range(nkb):
                hi, lo = _split_bf16(sp[:, kb * tk:(kb + 1) * tk])
                hlbuf[slot, h, kb] = jnp.concatenate([hi, lo], axis=1)

    def stage_weights(slot, qi, valid, masked):
        rows = rows_of(qi)
        for h in range(2):
            old = laterbuf[h, rows, :]
            lat = jnp.zeros_like(old) if masked else old
            for kb in range(nkb - 1, -1, -1):
                cum = jnp.dot(hlbuf[slot, h, kb], tt, preferred_element_type=F32)
                cols = slice(kb * tk, (kb + 1) * tk)
                w = jnp.exp2(lbbuf[slot, h, :, cols] - (lat + cum[:, :tk]))
                if masked:
                    w = jnp.where(causal[:, cols], w, 0.0)
                wbuf[slot, h, :, cols] = w.astype(BF16)
                lat = lat + cum[:, tk:]
            laterbuf[h, rows, :] = jnp.where(valid, lat, old)

    def stage_pv(slot, qi, kj, valid, masked):
        rows = rows_of(qi)
        v = v_ref[keys_of(kj), :]
        pv = [jnp.dot(wbuf[slot, h], v, preferred_element_type=F32) for h in range(2)]
        acc = jnp.where(valid, jnp.where(first_head, pv[0], pv[1]), 0.0)
        o_ref[rows, :] = o_ref[rows, :] + acc

    def n_trips(n_items):
        return (n_items + 3 + 1) // 2

    def run_pass(masked, n_items, first, advance, first_trip):
        def step(i, parity, coords):
            c1, c2, c3, c4 = coords

            def valid(s):
                return jnp.logical_and(i - s >= 0, i - s < n_items)

            stage_pv((parity + 3) % 2, c4[0], c4[1], valid(3), masked)
            stage_weights((parity + 2) % 2, c3[0], valid(2), masked)
            stage_softplus((parity + 1) % 2, masked)
            stage_qk(parity, c1[0], c1[1])
            nxt = advance(*c1)
            more = i + 1 < n_items
            c1n = (jnp.where(more, nxt[0], c1[0]), jnp.where(more, nxt[1], c1[1]))
            return c1n, c1, c2, c3

        def two_steps(i2, coords):
            coords = step(2 * i2 + 1, 1, step(2 * i2, 0, coords))
            decode_trip(first_trip + i2)
            return coords

        lax.fori_loop(0, n_trips(n_items), two_steps, (first,) * 4)

    def next_diagonal(qi, kj):
        return qi + 1, kj + 1

    def next_below(qi, kj):
        wrap = kj == 0
        return jnp.where(wrap, qi + 1, qi), jnp.where(wrap, qi, kj - 1)

    n_below = nq * (nq - 1) // 2
    assert n_trips(nq) + n_trips(n_below) >= chunks_per_step
    run_pass(True, nq, (jnp.int32(0), jnp.int32(0)), next_diagonal, 0)
    run_pass(False, n_below, (jnp.int32(1), jnp.int32(0)), next_below, n_trips(nq))


def _post_kernel(x_ref, attn_ref, cmix_ref, p_ref, gattn_ref, wout_ref, gffn_ref, wg_ref, wu_ref,
                 wd_ref, gple_ref, wpg_ref, wpp_ref, o_ref):
    att_w = attn_ref.shape[1]
    mix_a = _rms(attn_ref[...], gattn_ref[...]).astype(BF16)
    h = (x_ref[...]
         + jnp.dot(mix_a, wout_ref[0:att_w, :], preferred_element_type=F32)
         + jnp.dot(cmix_ref[...], wout_ref[att_w:, :], preferred_element_type=F32))
    hn = _rms(h, gffn_ref[...]).astype(BF16)
    g = jnp.dot(hn, wg_ref[...], preferred_element_type=F32)
    u = jnp.dot(hn, wu_ref[...], preferred_element_type=F32)
    act = (g * jax.nn.sigmoid(g) * u).astype(BF16)
    h = h + jnp.dot(act, wd_ref[...], preferred_element_type=F32)
    hp = _rms(h, gple_ref[...]).astype(BF16)
    gate = jax.nn.sigmoid(jnp.dot(hp, wpg_ref[...], preferred_element_type=F32))
    proj = jnp.dot(p_ref[...].astype(BF16), wpp_ref[...], preferred_element_type=F32)
    o_ref[...] = h + proj * gate


def _resident(shape):
    return pl.BlockSpec(shape, lambda *_: (0,) * len(shape), pipeline_mode=pl.Buffered(1))


def _row_spec(tm, width):
    return pl.BlockSpec((tm, width), lambda i: (i, 0))


def _params(*semantics):
    return pltpu.CompilerParams(dimension_semantics=semantics, vmem_limit_bytes=VMEM_LIMIT_BYTES)


def _inproj_weight_specs(d_model, w_total, att_w):
    return [_resident((1, d_model)), _resident((d_model, w_total)), _resident((1, att_w)),
            _resident((1, att_w)), _resident((V7X_MXU_DIM, V7X_MXU_DIM)),
            _resident((3, att_w)), _resident((1, att_w))]


def _inproj_prompt(x, seq_len, weights):
    m, d_model = x.shape
    w_total = weights[1].shape[1]
    att_w = weights[2].shape[1]
    tm = ROW_TILE_IN
    tiles_per_seq = seq_len // tm
    n_seq = m // seq_len
    rows = lambda dt: jax.ShapeDtypeStruct((m, att_w), dt)
    row_spec = _row_spec(tm, att_w)
    t_shape = jax.ShapeDtypeStruct((n_seq, att_w, seq_len), F32)
    t_spec = pl.BlockSpec((1, att_w, tm), lambda i: (i // tiles_per_seq, 0, i % tiles_per_seq))
    return pl.pallas_call(
        functools.partial(_inproj_prompt_kernel, tiles_per_seq),
        grid=(m // tm,),
        in_specs=[_row_spec(tm, d_model)] + _inproj_weight_specs(d_model, w_total, att_w),
        out_specs=[row_spec, t_spec, t_spec, row_spec, row_spec, row_spec,
                   pl.BlockSpec((1, V7X_SUBLANES, att_w), lambda i: (i // tiles_per_seq, 0, 0))],
        out_shape=[rows(BF16), t_shape, t_shape, rows(BF16), rows(BF16), rows(BF16),
                   jax.ShapeDtypeStruct((n_seq, V7X_SUBLANES, att_w), F32)],
        scratch_shapes=[pltpu.VMEM((tm + V7X_SUBLANES, att_w), F32)],
        compiler_params=_params("arbitrary"),
        name="inproj_prompt",
    )(x, *weights)


def _inproj_sample(x, s0, s1, weights):
    m, d_model = x.shape
    w_total = weights[1].shape[1]
    att_w = weights[2].shape[1]
    rows = lambda dt: jax.ShapeDtypeStruct((m, att_w), dt)
    return pl.pallas_call(
        _inproj_sample_kernel,
        grid=(1,),
        in_specs=[_row_spec(m, d_model)] + _inproj_weight_specs(d_model, w_total, att_w)
        + [_row_spec(m, att_w)] * 2,
        out_specs=[_row_spec(m, att_w)] * 5,
        out_shape=[rows(F32), rows(F32), rows(F32), rows(BF16), rows(F32)],
        compiler_params=_params("arbitrary"),
        name="inproj_sample",
    )(x, *weights, s0, s1)


def _cumsum_weights(tk):
    later = (jnp.arange(tk)[:, None] > jnp.arange(tk)[None, :]).astype(BF16)
    half = jnp.concatenate([later, jnp.ones((tk, tk), BF16)], axis=1)
    return jnp.concatenate([half, half], axis=0)


def _attention(q, k, v, q2, att_bias, cache_k, cache_v, page_table, n_prompt, seq_len):
    m, att_w = q.shape
    n_sample = q2.shape[0]
    n_pages = page_table.shape[1]
    _, page, n_heads, head_dim = cache_k.shape
    tq, ck, tk = ATT_TQ, ATT_CK, ATT_TK
    ch = DEC_CHUNK_PAGES
    pair_w = 2 * HEAD_DIM
    n_steps = n_prompt * (att_w // pair_w)
    assert pair_w == V7X_LANES and tq == ck and ck % tk == 0 and seq_len % tq == 0
    assert n_pages % ch == 0 and page == tk and n_heads == V7X_SUBLANES and n_sample % n_steps == 0
    pool_k = jnp.transpose(cache_k, (0, 2, 3, 1))
    pool_v = jnp.transpose(cache_v, (0, 2, 3, 1))
    dbias = jnp.tile(att_bias.reshape(n_heads, 1), (ch, page))
    tt = _cumsum_weights(tk)
    seq_block = pl.BlockSpec((seq_len, pair_w), lambda b, hp, pt: (b, hp))
    whole = lambda shape: pl.BlockSpec(shape, lambda b, hp, pt: (0,) * len(shape))
    page_buf = pltpu.VMEM((2, ch, n_heads, head_dim, page), F32)
    grid_spec = pltpu.PrefetchScalarGridSpec(
        num_scalar_prefetch=1,
        grid=(n_prompt, att_w // pair_w),
        in_specs=[pl.BlockSpec(memory_space=pltpu.SMEM), seq_block, seq_block, seq_block,
                  whole(tt.shape), whole((n_sample, 1, att_w)), whole(dbias.shape),
                  pl.BlockSpec(memory_space=pl.ANY), pl.BlockSpec(memory_space=pl.ANY)],
        out_specs=[seq_block, whole((n_sample, 1, att_w))],
        scratch_shapes=[pltpu.VMEM((2, 2, tq, ck), F32),
                        pltpu.VMEM((2, 2, tq, ck), F32),
                        pltpu.VMEM((2, 2, ck // tk, tq, 2 * tk), BF16),
                        pltpu.VMEM((2, 2, tq, ck), BF16),
                        pltpu.VMEM((2, seq_len, pair_w), F32),
                        page_buf, page_buf,
                        pltpu.VMEM((att_w, page), F32),
                        pltpu.VMEM((n_heads, page), F32),
                        pltpu.VMEM((att_w, page), F32),
                        pltpu.SemaphoreType.DMA((2, 2))],
    )
    attn, attn2 = pl.pallas_call(
        functools.partial(_attention_kernel, n_pages, n_sample // n_steps),
        grid_spec=grid_spec,
        out_shape=[jax.ShapeDtypeStruct((m, att_w), F32),
                   jax.ShapeDtypeStruct((n_sample, 1, att_w), F32)],
        compiler_params=_params("arbitrary", "arbitrary"),
        name="attention",
    )(page_table.reshape(-1), att_bias, q, k, v, tt, q2.reshape(n_sample, 1, att_w), dbias,
      pool_k, pool_v)
    return attn, attn2.reshape(n_sample, att_w)


def _post(x, attn, cmix, p, weights):
    m, d_model = x.shape
    tm = min(ROW_TILE_POST, m)
    specs = [_row_spec(tm, d_model), _row_spec(tm, attn.shape[1]), _row_spec(tm, cmix.shape[1]),
             _row_spec(tm, p.shape[1])] + [_resident(w.shape) for w in weights]
    return pl.pallas_call(
        _post_kernel,
        grid=(m // tm,),
        in_specs=specs,
        out_specs=_row_spec(tm, d_model),
        out_shape=jax.ShapeDtypeStruct((m, d_model), F32),
        compiler_params=_params("arbitrary"),
        name="post_mixer",
    )(x, attn, cmix, p, *weights)


def kernel(x_prompt, x_sample, cache_k, cache_v, state_conv, page_table, p_prompt, p_sample,
           norm_mix, w_in, q_norm, k_norm, att_bias, conv_w, attn_out_norm, conv_out_norm, w_out,
           norm_ffn, w_gate, w_up, w_down, ple_norm, w_ple_gate, w_ple_proj):
    bp, sp, d_model = x_prompt.shape
    bs, ts, _ = x_sample.shape
    depth = w_in.shape[0]
    att_w = w_in.shape[2] // 6
    n_heads = att_w // HEAD_DIM
    assert ts == 1 and conv_w.shape[1] == 3 and conv_w.shape[2] == att_w

    gidx = jnp.arange(V7X_MXU_DIM) // HEAD_DIM
    group_ones = (gidx[:, None] == gidx[None, :]).astype(BF16)
    row = lambda vec: vec.reshape(1, -1)

    hp = x_prompt.reshape(bp * sp, d_model)
    hs = x_sample.reshape(bs * ts, d_model)
    k_p, v_p, c_p, k_s, v_s, c_s = [], [], [], [], [], []
    for i in range(depth):
        in_w = (row(norm_mix[i]), w_in[i].astype(BF16), row(jnp.tile(q_norm[i], n_heads)),
                row(jnp.tile(k_norm[i], n_heads)), group_ones, conv_w[i], row(conv_out_norm[i]))
        post_w = (row(attn_out_norm[i]), w_out[i].astype(BF16), row(norm_ffn[i]),
                  w_gate[i].astype(BF16), w_up[i].astype(BF16), w_down[i].astype(BF16),
                  row(ple_norm[i]), w_ple_gate[i].astype(BF16), w_ple_proj[i].astype(BF16))

        s0 = state_conv[i][:, 0, :]
        s1 = state_conv[i][:, 1, :]
        q, kt, vt, kb, vb, cmix, tail = _inproj_prompt(hp, sp, in_w)
        q2, k2, v2, cmix2, u2 = _inproj_sample(hs, s0, s1, in_w)
        attn, attn2 = _attention(q, kb, vb, q2, att_bias[i], cache_k[i], cache_v[i], page_table,
                                 bp, sp)

        hp = _post(hp, attn, cmix, p_prompt[i].reshape(bp * sp, -1), post_w)
        from_t = lambda t: jnp.transpose(t.reshape(bp, n_heads, HEAD_DIM, sp), (0, 3, 1, 2))
        k_p.append(from_t(kt))
        v_p.append(from_t(vt))
        c_p.append(tail[:, V7X_SUBLANES - 2:, :])

        hs = _post(hs, attn2, cmix2, p_sample[i].reshape(bs * ts, -1), post_w)
        k_s.append(k2.reshape(bs, ts, n_heads, HEAD_DIM))
        v_s.append(v2.reshape(bs, ts, n_heads, HEAD_DIM))
        c_s.append(jnp.stack([s1, u2], axis=1))

    return (hp.reshape(bp, sp, d_model), hs.reshape(bs, ts, d_model),
            jnp.stack(k_p), jnp.stack(v_p), jnp.stack(c_p),
            jnp.stack(k_s), jnp.stack(v_s), jnp.stack(c_s))
```

```python
import functools

import jax
import jax.numpy as jnp
from jax import lax
from jax.experimental import pallas as pl
from jax.experimental.pallas import tpu as pltpu

F32 = jnp.float32
BF16 = jnp.bfloat16

EPS = 1e-6
HEAD_DIM = 64
LOG2E = 1.4426950408889634

V7X_LANES = 128
V7X_SUBLANES = 8
V7X_MXU_DIM = 256
V7X_VMEM_BYTES = 64 * 1024 * 1024
VMEM_LIMIT_BYTES = V7X_VMEM_BYTES * 7 // 8

ROW_TILE_IN = 512
ROW_TILE_POST = 256
ATT_TQ = 256
ATT_CK = 256
ATT_TK = V7X_LANES
DEC_CHUNK_PAGES = 8

_NT = (((1,), (1,)), ((), ()))


def _rms(x, g):
    ms = jnp.mean(x * x, axis=-1, keepdims=True)
    return x * lax.rsqrt(ms + EPS) * g


def _softplus2(z2):
    return jnp.maximum(z2, 0.0) + jnp.log2(1.0 + jnp.exp2(-jnp.abs(z2)))


def _split_bf16(x):
    hi = x.astype(BF16)
    lo = (x - hi.astype(F32)).astype(BF16)
    return hi, lo


def _head_rms(t, gn, group_ones):
    sq = (t * t).astype(BF16)
    w = group_ones.shape[0]
    ss = jnp.concatenate(
        [jnp.dot(sq[:, i:i + w], group_ones, preferred_element_type=F32)
         for i in range(0, t.shape[1], w)], axis=1)
    return t * lax.rsqrt(ss * (1.0 / HEAD_DIM) + EPS) * gn


def _in_proj(x_ref, gmix_ref, win_ref, qn_ref, kn_ref, gones_ref):
    w = qn_ref.shape[1]
    hn = _rms(x_ref[...], gmix_ref[...]).astype(BF16)

    def sec(i):
        return jnp.dot(hn, win_ref[:, i * w:(i + 1) * w], preferred_element_type=F32)

    gones = gones_ref[...]
    q = _head_rms(sec(0), qn_ref[...], gones) * (HEAD_DIM ** -0.5 * LOG2E)
    k = _head_rms(sec(1), kn_ref[...], gones)
    v = sec(2)
    b_gate = sec(3)
    u = sec(4) * sec(5)
    return q, k, v, b_gate, u


def _inproj_prompt_kernel(tiles_per_seq, x_ref, gmix_ref, win_ref, qn_ref, kn_ref, gones_ref,
                          cw_ref, cn_ref,
                          q_ref, kt_ref, vt_ref, kb_ref, vb_ref, cmix_ref, tail_ref, ubuf):
    tm = x_ref.shape[0]
    pad = V7X_SUBLANES
    q, k, v, b_gate, u = _in_proj(x_ref, gmix_ref, win_ref, qn_ref, kn_ref, gones_ref)
    q_ref[...] = q.astype(BF16)
    kt_ref[0] = k.T
    vt_ref[0] = v.T
    kb_ref[...] = k.astype(BF16)
    vb_ref[...] = v.astype(BF16)

    @pl.when(pl.program_id(0) % tiles_per_seq == 0)
    def _():
        ubuf[0:pad, :] = jnp.zeros((pad, ubuf.shape[1]), F32)

    ubuf[pad:pad + tm, :] = u
    cw = cw_ref[...]
    y = (cw[0:1] * ubuf[pad - 2:pad - 2 + tm, :] + cw[1:2] * ubuf[pad - 1:pad - 1 + tm, :]
         + cw[2:3] * u)
    cmix_ref[...] = _rms(b_gate * y, cn_ref[...]).astype(BF16)
    tail = u[tm - pad:, :]
    ubuf[0:pad, :] = tail
    tail_ref[0] = tail


def _inproj_sample_kernel(x_ref, gmix_ref, win_ref, qn_ref, kn_ref, gones_ref, cw_ref, cn_ref,
                          s0_ref, s1_ref,
                          q_ref, k_ref, v_ref, cmix_ref, u_ref):
    q, k, v, b_gate, u = _in_proj(x_ref, gmix_ref, win_ref, qn_ref, kn_ref, gones_ref)
    q_ref[...] = q
    k_ref[...] = k
    v_ref[...] = v
    u_ref[...] = u
    cw = cw_ref[...]
    y = cw[0:1] * s0_ref[...] + cw[1:2] * s1_ref[...] + cw[2:3] * u
    cmix_ref[...] = _rms(b_gate * y, cn_ref[...]).astype(BF16)


class _PagedAttention:
    def __init__(self, n_pages, pt_ref, q2_ref, dbias_ref, tt, ck_hbm, cv_hbm, o2_ref,
                 kbuf, vbuf, dacc, dlater, qcol, sem):
        self.n_pages, self.pt_ref, self.q2_ref, self.dbias_ref, self.tt = n_pages, pt_ref, q2_ref, dbias_ref, tt
        self.ck_hbm, self.cv_hbm, self.o2_ref = ck_hbm, cv_hbm, o2_ref
        self.kbuf, self.vbuf, self.dacc, self.dlater, self.qcol, self.sem = kbuf, vbuf, dacc, dlater, qcol, sem
        _, self.n_heads, self.hd, self.page = ck_hbm.shape
        self.ch = kbuf.shape[1]
        self.n_chunks = n_pages // self.ch
        self.n_total = o2_ref.shape[0] * self.n_chunks

    def _copies(self, g, slot):
        seq = g // self.n_chunks
        c = g % self.n_chunks
        base = seq * self.n_pages + (self.n_chunks - 1 - c) * self.ch
        copies = []
        for p in range(self.ch):
            pid = self.pt_ref[base + p]
            copies.append(pltpu.make_async_copy(self.ck_hbm.at[pid], self.kbuf.at[slot, p],
                                                self.sem.at[0, slot]))
            copies.append(pltpu.make_async_copy(self.cv_hbm.at[pid], self.vbuf.at[slot, p],
                                                self.sem.at[1, slot]))
        return copies

    def prime(self):
        for cp in self._copies(0, 0):
            cp.start()

    def begin(self, g):
        slot = g % 2
        for cp in self._copies(g, slot):
            cp.wait()

        @pl.when(g + 1 < self.n_total)
        def _():
            for cp in self._copies(g + 1, 1 - slot):
                cp.start()

        @pl.when(g % self.n_chunks == 0)
        def _():
            q_row = self.q2_ref[g // self.n_chunks]
            self.qcol[...] = jnp.broadcast_to(q_row, (self.page, q_row.shape[1])).T
            self.dacc[...] = jnp.zeros(self.dacc.shape, F32)
            self.dlater[...] = jnp.zeros(self.dlater.shape, F32)

    def compute(self, g):
        n_heads, hd, page, ch = self.n_heads, self.hd, self.page, self.ch
        kbuf, vbuf, dacc, dlater, qcol = self.kbuf, self.vbuf, self.dacc, self.dlater, self.qcol
        slot = g % 2
        sub = lax.broadcasted_iota(jnp.int32, (n_heads, page), 0)
        zs = []
        for p in range(ch):
            zp = jnp.zeros((n_heads, page), F32)
            for h in range(n_heads):
                prod = kbuf[slot, p, h] * qcol[h * hd:(h + 1) * hd, :]
                zp = jnp.where(sub == h, jnp.sum(prod, axis=0, keepdims=True), zp)
            zs.append(zp)
        z = jnp.concatenate(zs, axis=0) + self.dbias_ref[...] * LOG2E
        sp = _softplus2(z)
        hi, lo = _split_bf16(sp)
        cum = jnp.dot(jnp.concatenate([hi, lo], axis=1), self.tt, preferred_element_type=F32)
        later = dlater[...]
        laters = [None] * ch
        for p in range(ch - 1, -1, -1):
            laters[p] = later
            later = later + cum[p * n_heads:(p + 1) * n_heads, page:]
        dlater[...] = later
        w = jnp.exp2((z - sp) - (jnp.concatenate(laters, axis=0) + cum[:, :page]))
        for h in range(n_heads):
            rows = slice(h * hd, (h + 1) * hd)
            a = dacc[rows, :]
            for p in range(ch):
                r = p * n_heads + h
                a = a + w[r:r + 1, :] * vbuf[slot, p, h]
            dacc[rows, :] = a

    def end(self, g):
        @pl.when(g % self.n_chunks == self.n_chunks - 1)
        def _():
            hi_a, lo_a = _split_bf16(self.dacc[...])
            ones_row = jnp.ones((V7X_SUBLANES, self.page), BF16)
            out = (lax.dot_general(ones_row, hi_a, _NT, preferred_element_type=F32)
                   + lax.dot_general(ones_row, lo_a, _NT, preferred_element_type=F32))
            self.o2_ref[g // self.n_chunks] = out[0:1]


def _attention_kernel(n_pages, seqs_per_step, pt_ref, bias_ref, q_ref, k_ref, v_ref, tt_ref,
                      q2_ref, dbias_ref, ck_hbm, cv_hbm, o_ref, o2_ref,
                      zbuf, lbbuf, hlbuf, wbuf, laterbuf, kbuf, vbuf, dacc, dlater, qcol, sem):
    hp = pl.program_id(1)
    seq_len, lanes = q_ref.shape
    tq, ck, tk = ATT_TQ, ATT_CK, ATT_TK
    nq = seq_len // tq
    nkb = ck // tk
    first_head = lax.broadcasted_iota(jnp.int32, (tq, lanes), 1) < HEAD_DIM
    bias = (bias_ref[2 * hp] * LOG2E, bias_ref[2 * hp + 1] * LOG2E)
    tt = tt_ref[...]
    causal = (lax.broadcasted_iota(jnp.int32, (tq, ck), 1)
              < lax.broadcasted_iota(jnp.int32, (tq, ck), 0))

    @pl.when(jnp.logical_and(pl.program_id(0) == 0, hp == 0))
    def _():
        zbuf[...] = jnp.zeros(zbuf.shape, zbuf.dtype)
        lbbuf[...] = jnp.zeros(lbbuf.shape, lbbuf.dtype)
        hlbuf[...] = jnp.zeros(hlbuf.shape, hlbuf.dtype)
        wbuf[...] = jnp.zeros(wbuf.shape, wbuf.dtype)
        laterbuf[...] = jnp.zeros(laterbuf.shape, laterbuf.dtype)

    o_ref[...] = jnp.zeros(o_ref.shape, o_ref.dtype)

    dec = _PagedAttention(n_pages, pt_ref, q2_ref, dbias_ref, tt, ck_hbm, cv_hbm, o2_ref,
                          kbuf, vbuf, dacc, dlater, qcol, sem)
    step_id = pl.program_id(0) * pl.num_programs(1) + hp
    chunks_per_step = seqs_per_step * dec.n_chunks

    @pl.when(step_id == 0)
    def _():
        dec.prime()

    def rows_of(qi):
        return pl.ds(pl.multiple_of(qi * tq, tq), tq)

    def keys_of(kj):
        return pl.ds(pl.multiple_of(kj * ck, ck), ck)

    def stage_qk(slot, qi, kj):
        q = q_ref[rows_of(qi), :]
        k = k_ref[keys_of(kj), :]
        zero = jnp.zeros_like(q)
        for h in range(2):
            qh = jnp.where(first_head, q, zero) if h == 0 else jnp.where(first_head, zero, q)
            zbuf[slot, h] = lax.dot_general(qh, k, _NT, preferred_element_type=F32) + bias[h]

    def stage_softplus(slot, masked):
        for h in range(2):
            z = zbuf[slot, h]
            sp = _softplus2(z)
            if masked:
                sp = jnp.where(causal, sp, 0.0)
            lbbuf[slot, h] = z - sp
            for kb in range(nkb):
                hi, lo = _split_bf16(sp[:, kb * tk:(kb + 1) * tk])
                hlbuf[slot, h, kb] = jnp.concatenate([hi, lo], axis=1)

    def stage_weights(slot, qi, valid, masked):
        rows = rows_of(qi)
        for h in range(2):
            old = laterbuf[h, rows, :]
            lat = jnp.zeros_like(old) if masked else old
            for kb in range(nkb - 1, -1, -1):
                cum = jnp.dot(hlbuf[slot, h, kb], tt, preferred_element_type=F32)
                cols = slice(kb * tk, (kb + 1) * tk)
                w = jnp.exp2(lbbuf[slot, h, :, cols] - (lat + cum[:, :tk]))
                if masked:
                    w = jnp.where(causal[:, cols], w, 0.0)
                wbuf[slot, h, :, cols] = w.astype(BF16)
                lat = lat + cum[:, tk:]
            laterbuf[h, rows, :] = jnp.where(valid, lat, old)

    def stage_pv(slot, qi, kj, valid, masked):
        rows = rows_of(qi)
        v = v_ref[keys_of(kj), :]
        pv = [jnp.dot(wbuf[slot, h], v, preferred_element_type=F32) for h in range(2)]
        acc = jnp.where(valid, jnp.where(first_head, pv[0], pv[1]), 0.0)
        o_ref[rows, :] = o_ref[rows, :] + acc

    def n_trips(n_items):
        return (n_items + 3 + 1) // 2

    def run_pass(masked, n_items, first, advance, first_trip):
        def step(i, parity, coords):
            c1, c2, c3, c4 = coords

            def valid(s):
                return jnp.logical_and(i - s >= 0, i - s < n_items)

            stage_pv((parity + 3) % 2, c4[0], c4[1], valid(3), masked)
            stage_weights((parity + 2) % 2, c3[0], valid(2), masked)
            stage_softplus((parity + 1) % 2, masked)
            stage_qk(parity, c1[0], c1[1])
            nxt = advance(*c1)
            more = i + 1 < n_items
            c1n = (jnp.where(more, nxt[0], c1[0]), jnp.where(more, nxt[1], c1[1]))
            return c1n, c1, c2, c3

        def two_steps(i2, coords):
            return step(2 * i2 + 1, 1, step(2 * i2, 0, coords))

        def two_steps_and_chunk(i2, coords):
            g = step_id * chunks_per_step + first_trip + i2
            dec.begin(g)
            coords = two_steps(i2, coords)
            dec.compute(g)
            dec.end(g)
            return coords

        n_with = max(0, min(n_trips(n_items), chunks_per_step - first_trip))
        coords = lax.fori_loop(0, n_with, two_steps_and_chunk, (first,) * 4)
        lax.fori_loop(n_with, n_trips(n_items), two_steps, coords)

    def next_diagonal(qi, kj):
        return qi + 1, kj + 1

    def next_below(qi, kj):
        wrap = kj == 0
        return jnp.where(wrap, qi + 1, qi), jnp.where(wrap, qi, kj - 1)

    n_below = nq * (nq - 1) // 2
    assert n_trips(nq) + n_trips(n_below) >= chunks_per_step
    run_pass(True, nq, (jnp.int32(0), jnp.int32(0)), next_diagonal, 0)
    run_pass(False, n_below, (jnp.int32(1), jnp.int32(0)), next_below, n_trips(nq))


def _post_kernel(x_ref, attn_ref, cmix_ref, p_ref, gattn_ref, wout_ref, gffn_ref, wg_ref, wu_ref,
                 wd_ref, gple_ref, wpg_ref, wpp_ref, o_ref):
    att_w = attn_ref.shape[1]
    mix_a = _rms(attn_ref[...], gattn_ref[...]).astype(BF16)
    h = (x_ref[...]
         + jnp.dot(mix_a, wout_ref[0:att_w, :], preferred_element_type=F32)
         + jnp.dot(cmix_ref[...], wout_ref[att_w:, :], preferred_element_type=F32))
    hn = _rms(h, gffn_ref[...]).astype(BF16)
    g = jnp.dot(hn, wg_ref[...], preferred_element_type=F32)
    u = jnp.dot(hn, wu_ref[...], preferred_element_type=F32)
    act = (g * jax.nn.sigmoid(g) * u).astype(BF16)
    h = h + jnp.dot(act, wd_ref[...], preferred_element_type=F32)
    hp = _rms(h, gple_ref[...]).astype(BF16)
    gate = jax.nn.sigmoid(jnp.dot(hp, wpg_ref[...], preferred_element_type=F32))
    proj = jnp.dot(p_ref[...].astype(BF16), wpp_ref[...], preferred_element_type=F32)
    o_ref[...] = h + proj * gate


def _resident(shape):
    return pl.BlockSpec(shape, lambda *_: (0,) * len(shape), pipeline_mode=pl.Buffered(1))


def _whole(shape):
    return pl.BlockSpec(shape, lambda *_: (0,) * len(shape))


def _row_spec(tm, width):
    return pl.BlockSpec((tm, width), lambda i: (i, 0))


def _params(*semantics):
    return pltpu.CompilerParams(dimension_semantics=semantics, vmem_limit_bytes=VMEM_LIMIT_BYTES)


def _inproj_weight_specs(d_model, w_total, att_w):
    return [_resident((1, d_model)), _resident((d_model, w_total)), _resident((1, att_w)),
            _resident((1, att_w)), _resident((V7X_MXU_DIM, V7X_MXU_DIM)),
            _resident((3, att_w)), _resident((1, att_w))]


def _inproj_prompt(x, seq_len, weights):
    m, d_model = x.shape
    w_total = weights[1].shape[1]
    att_w = weights[2].shape[1]
    tm = ROW_TILE_IN
    tiles_per_seq = seq_len // tm
    n_seq = m // seq_len
    rows = lambda dt: jax.ShapeDtypeStruct((m, att_w), dt)
    row_spec = _row_spec(tm, att_w)
    t_shape = jax.ShapeDtypeStruct((n_seq, att_w, seq_len), F32)
    t_spec = pl.BlockSpec((1, att_w, tm), lambda i: (i // tiles_per_seq, 0, i % tiles_per_seq))
    return pl.pallas_call(
        functools.partial(_inproj_prompt_kernel, tiles_per_seq),
        grid=(m // tm,),
        in_specs=[_row_spec(tm, d_model)] + _inproj_weight_specs(d_model, w_total, att_w),
        out_specs=[row_spec, t_spec, t_spec, row_spec, row_spec, row_spec,
                   pl.BlockSpec((1, V7X_SUBLANES, att_w), lambda i: (i // tiles_per_seq, 0, 0))],
        out_shape=[rows(BF16), t_shape, t_shape, rows(BF16), rows(BF16), rows(BF16),
                   jax.ShapeDtypeStruct((n_seq, V7X_SUBLANES, att_w), F32)],
        scratch_shapes=[pltpu.VMEM((tm + V7X_SUBLANES, att_w), F32)],
        compiler_params=_params("arbitrary"),
        name="inproj_prompt",
    )(x, *weights)


def _inproj_sample(x, s0, s1, weights):
    m, d_model = x.shape
    w_total = weights[1].shape[1]
    att_w = weights[2].shape[1]
    rows = lambda dt: jax.ShapeDtypeStruct((m, att_w), dt)
    return pl.pallas_call(
        _inproj_sample_kernel,
        grid=(1,),
        in_specs=[_row_spec(m, d_model)] + _inproj_weight_specs(d_model, w_total, att_w)
        + [_row_spec(m, att_w)] * 2,
        out_specs=[_row_spec(m, att_w)] * 5,
        out_shape=[rows(F32), rows(F32), rows(F32), rows(BF16), rows(F32)],
        compiler_params=_params("arbitrary"),
        name="inproj_sample",
    )(x, *weights, s0, s1)


def _cumsum_weights(tk):
    later = (jnp.arange(tk)[:, None] > jnp.arange(tk)[None, :]).astype(BF16)
    half = jnp.concatenate([later, jnp.ones((tk, tk), BF16)], axis=1)
    return jnp.concatenate([half, half], axis=0)


def _paged_operands(q2, att_bias, pool_k, pool_v, page_table):
    n, att_w = q2.shape
    _, n_heads, head_dim, page = pool_k.shape
    ch = DEC_CHUNK_PAGES
    assert page_table.shape[1] % ch == 0 and page == ATT_TK and n_heads == V7X_SUBLANES
    dbias = jnp.tile(att_bias.reshape(n_heads, 1), (ch, page))
    tt = _cumsum_weights(page)
    page_buf = pltpu.VMEM((2, ch, n_heads, head_dim, page), F32)
    return dict(
        prefetch=page_table.reshape(-1),
        operands=(q2.reshape(n, 1, att_w), dbias, tt, pool_k, pool_v),
        in_specs=[_whole((n, 1, att_w)), _whole(dbias.shape), _whole(tt.shape),
                  pl.BlockSpec(memory_space=pl.ANY), pl.BlockSpec(memory_space=pl.ANY)],
        out_spec=_whole((n, 1, att_w)),
        out_shape=jax.ShapeDtypeStruct((n, 1, att_w), F32),
        scratch=[page_buf, page_buf,
                 pltpu.VMEM((att_w, page), F32),
                 pltpu.VMEM((n_heads, page), F32),
                 pltpu.VMEM((att_w, page), F32),
                 pltpu.SemaphoreType.DMA((2, 2))])


def _attention(q, k, v, att_bias, paged, n_prompt, seq_len):
    m, att_w = q.shape
    tq, ck, tk = ATT_TQ, ATT_CK, ATT_TK
    pair_w = 2 * HEAD_DIM
    n_steps = n_prompt * (att_w // pair_w)
    n_sample = paged["out_shape"].shape[0]
    assert pair_w == V7X_LANES and tq == ck and ck % tk == 0 and seq_len % tq == 0
    assert n_sample % n_steps == 0
    n_pages = paged["prefetch"].shape[0] // n_sample
    q2, dbias, tt, pool_k, pool_v = paged["operands"]
    seq_block = pl.BlockSpec((seq_len, pair_w), lambda b, hp, pt: (b, hp))
    grid_spec = pltpu.PrefetchScalarGridSpec(
        num_scalar_prefetch=1,
        grid=(n_prompt, att_w // pair_w),
        in_specs=[pl.BlockSpec(memory_space=pltpu.SMEM), seq_block, seq_block, seq_block,
                  paged["in_specs"][2], paged["in_specs"][0], paged["in_specs"][1],
                  paged["in_specs"][3], paged["in_specs"][4]],
        out_specs=[seq_block, paged["out_spec"]],
        scratch_shapes=[pltpu.VMEM((2, 2, tq, ck), F32),
                        pltpu.VMEM((2, 2, tq, ck), F32),
                        pltpu.VMEM((2, 2, ck // tk, tq, 2 * tk), BF16),
                        pltpu.VMEM((2, 2, tq, ck), BF16),
                        pltpu.VMEM((2, seq_len, pair_w), F32)]
        + paged["scratch"],
    )
    attn, attn2 = pl.pallas_call(
        functools.partial(_attention_kernel, n_pages, n_sample // n_steps),
        grid_spec=grid_spec,
        out_shape=[jax.ShapeDtypeStruct((m, att_w), F32), paged["out_shape"]],
        compiler_params=_params("arbitrary", "arbitrary"),
        name="attention",
    )(paged["prefetch"], att_bias, q, k, v, tt, q2, dbias, pool_k, pool_v)
    return attn, attn2.reshape(n_sample, att_w)


def _post(x, attn, cmix, p, weights):
    m, d_model = x.shape
    tm = min(ROW_TILE_POST, m)
    specs = [_row_spec(tm, d_model), _row_spec(tm, attn.shape[1]), _row_spec(tm, cmix.shape[1]),
             _row_spec(tm, p.shape[1])] + [_resident(w.shape) for w in weights]
    return pl.pallas_call(
        _post_kernel,
        grid=(m // tm,),
        in_specs=specs,
        out_specs=_row_spec(tm, d_model),
        out_shape=jax.ShapeDtypeStruct((m, d_model), F32),
        compiler_params=_params("arbitrary"),
        name="post_mixer",
    )(x, attn, cmix, p, *weights)


def kernel(x_prompt, x_sample, cache_k, cache_v, state_conv, page_table, p_prompt, p_sample,
           norm_mix, w_in, q_norm, k_norm, att_bias, conv_w, attn_out_norm, conv_out_norm, w_out,
           norm_ffn, w_gate, w_up, w_down, ple_norm, w_ple_gate, w_ple_proj):
    bp, sp, d_model = x_prompt.shape
    bs, ts, _ = x_sample.shape
    depth = w_in.shape[0]
    att_w = w_in.shape[2] // 6
    n_heads = att_w // HEAD_DIM
    assert ts == 1 and conv_w.shape[1] == 3 and conv_w.shape[2] == att_w

    gidx = jnp.arange(V7X_MXU_DIM) // HEAD_DIM
    group_ones = (gidx[:, None] == gidx[None, :]).astype(BF16)
    row = lambda vec: vec.reshape(1, -1)

    hp = x_prompt.reshape(bp * sp, d_model)
    hs = x_sample.reshape(bs * ts, d_model)
    k_p, v_p, c_p, k_s, v_s, c_s = [], [], [], [], [], []
    for i in range(depth):
        in_w = (row(norm_mix[i]), w_in[i].astype(BF16), row(jnp.tile(q_norm[i], n_heads)),
                row(jnp.tile(k_norm[i], n_heads)), group_ones, conv_w[i], row(conv_out_norm[i]))
        post_w = (row(attn_out_norm[i]), w_out[i].astype(BF16), row(norm_ffn[i]),
                  w_gate[i].astype(BF16), w_up[i].astype(BF16), w_down[i].astype(BF16),
                  row(ple_norm[i]), w_ple_gate[i].astype(BF16), w_ple_proj[i].astype(BF16))

        s0 = state_conv[i][:, 0, :]
        s1 = state_conv[i][:, 1, :]
        q, kt, vt, kb, vb, cmix, tail = _inproj_prompt(hp, sp, in_w)
        q2, k2, v2, cmix2, u2 = _inproj_sample(hs, s0, s1, in_w)
        pool_k = jnp.transpose(cache_k[i], (0, 2, 3, 1))
        pool_v = jnp.transpose(cache_v[i], (0, 2, 3, 1))
        paged = _paged_operands(q2, att_bias[i], pool_k, pool_v, page_table)

        attn, attn2 = _attention(q, kb, vb, att_bias[i], paged, bp, sp)
        hp = _post(hp, attn, cmix, p_prompt[i].reshape(bp * sp, -1), post_w)
        from_t = lambda t: jnp.transpose(t.reshape(bp, n_heads, HEAD_DIM, sp), (0, 3, 1, 2))
        k_p.append(from_t(kt))
        v_p.append(from_t(vt))
        c_p.append(tail[:, V7X_SUBLANES - 2:, :])

        hs = _post(hs, attn2, cmix2, p_sample[i].reshape(bs * ts, -1), post_w)
        k_s.append(k2.reshape(bs, ts, n_heads, HEAD_DIM))
        v_s.append(v2.reshape(bs, ts, n_heads, HEAD_DIM))
        c_s.append(jnp.stack([s1, u2], axis=1))

    return (hp.reshape(bp, sp, d_model), hs.reshape(bs, ts, d_model),
            jnp.stack(k_p), jnp.stack(v_p), jnp.stack(c_p),
            jnp.stack(k_s), jnp.stack(v_s), jnp.stack(c_s))
```

```python
import functools

import jax
import jax.numpy as jnp
from jax import lax
from jax.experimental import pallas as pl
from jax.experimental.pallas import tpu as pltpu

F32 = jnp.float32
BF16 = jnp.bfloat16

EPS = 1e-6
HEAD_DIM = 64
LOG2E = 1.4426950408889634

V7X_LANES = 128
V7X_SUBLANES = 8
V7X_MXU_DIM = 256
V7X_VMEM_BYTES = 64 * 1024 * 1024
VMEM_LIMIT_BYTES = V7X_VMEM_BYTES * 7 // 8

ROW_TILE_IN = 512
ROW_TILE_POST = 256
ATT_TQ = 256
ATT_CK = 256
ATT_TK = V7X_LANES
DEC_CHUNK_PAGES = 8
DEC_SLOTS = 3

_NT = (((1,), (1,)), ((), ()))


def _rms(x, g):
    ms = jnp.mean(x * x, axis=-1, keepdims=True)
    return x * lax.rsqrt(ms + EPS) * g


def _softplus2(z2):
    return jnp.maximum(z2, 0.0) + jnp.log2(1.0 + jnp.exp2(-jnp.abs(z2)))


def _split_bf16(x):
    hi = x.astype(BF16)
    lo = (x - hi.astype(F32)).astype(BF16)
    return hi, lo


def _head_rms(t, gn, group_ones):
    sq = (t * t).astype(BF16)
    w = group_ones.shape[0]
    ss = jnp.concatenate(
        [jnp.dot(sq[:, i:i + w], group_ones, preferred_element_type=F32)
         for i in range(0, t.shape[1], w)], axis=1)
    return t * lax.rsqrt(ss * (1.0 / HEAD_DIM) + EPS) * gn


def _in_proj(x_ref, gmix_ref, win_ref, qn_ref, kn_ref, gones_ref):
    w = qn_ref.shape[1]
    hn = _rms(x_ref[...], gmix_ref[...]).astype(BF16)

    def sec(i):
        return jnp.dot(hn, win_ref[:, i * w:(i + 1) * w], preferred_element_type=F32)

    gones = gones_ref[...]
    q = _head_rms(sec(0), qn_ref[...], gones) * (HEAD_DIM ** -0.5 * LOG2E)
    k = _head_rms(sec(1), kn_ref[...], gones)
    v = sec(2)
    b_gate = sec(3)
    u = sec(4) * sec(5)
    return q, k, v, b_gate, u


def _inproj_prompt_kernel(tiles_per_seq, x_ref, gmix_ref, win_ref, qn_ref, kn_ref, gones_ref,
                          cw_ref, cn_ref,
                          q_ref, kt_ref, vt_ref, kb_ref, vb_ref, cmix_ref, tail_ref, ubuf):
    tm = x_ref.shape[0]
    pad = V7X_SUBLANES
    q, k, v, b_gate, u = _in_proj(x_ref, gmix_ref, win_ref, qn_ref, kn_ref, gones_ref)
    q_ref[...] = q.astype(BF16)
    kt_ref[0] = k.T
    vt_ref[0] = v.T
    kb_ref[...] = k.astype(BF16)
    vb_ref[...] = v.astype(BF16)

    @pl.when(pl.program_id(0) % tiles_per_seq == 0)
    def _():
        ubuf[0:pad, :] = jnp.zeros((pad, ubuf.shape[1]), F32)

    ubuf[pad:pad + tm, :] = u
    cw = cw_ref[...]
    y = (cw[0:1] * ubuf[pad - 2:pad - 2 + tm, :] + cw[1:2] * ubuf[pad - 1:pad - 1 + tm, :]
         + cw[2:3] * u)
    cmix_ref[...] = _rms(b_gate * y, cn_ref[...]).astype(BF16)
    tail = u[tm - pad:, :]
    ubuf[0:pad, :] = tail
    tail_ref[0] = tail


def _inproj_sample_kernel(x_ref, gmix_ref, win_ref, qn_ref, kn_ref, gones_ref, cw_ref, cn_ref,
                          s0_ref, s1_ref,
                          q_ref, k_ref, v_ref, cmix_ref, u_ref):
    q, k, v, b_gate, u = _in_proj(x_ref, gmix_ref, win_ref, qn_ref, kn_ref, gones_ref)
    q_ref[...] = q
    k_ref[...] = k
    v_ref[...] = v
    u_ref[...] = u
    cw = cw_ref[...]
    y = cw[0:1] * s0_ref[...] + cw[1:2] * s1_ref[...] + cw[2:3] * u
    cmix_ref[...] = _rms(b_gate * y, cn_ref[...]).astype(BF16)


class _PagedAttention:
    def __init__(self, n_pages, pt_ref, q2_ref, dbias_ref, tt, ck_hbm, cv_hbm, o2_ref,
                 kbuf, vbuf, dacc, dlater, qcol, sem):
        self.n_pages, self.pt_ref, self.q2_ref, self.dbias_ref, self.tt = n_pages, pt_ref, q2_ref, dbias_ref, tt
        self.ck_hbm, self.cv_hbm, self.o2_ref = ck_hbm, cv_hbm, o2_ref
        self.kbuf, self.vbuf, self.dacc, self.dlater, self.qcol, self.sem = kbuf, vbuf, dacc, dlater, qcol, sem
        _, self.n_heads, self.hd, self.page = ck_hbm.shape
        self.n_slots, self.ch = kbuf.shape[:2]
        self.n_chunks = n_pages // self.ch
        self.n_total = o2_ref.shape[0] * self.n_chunks

    def _copies(self, g, slot):
        seq = g // self.n_chunks
        c = g % self.n_chunks
        base = seq * self.n_pages + (self.n_chunks - 1 - c) * self.ch
        copies = []
        for p in range(self.ch):
            pid = self.pt_ref[base + p]
            copies.append(pltpu.make_async_copy(self.ck_hbm.at[pid], self.kbuf.at[slot, p],
                                                self.sem.at[0, slot]))
            copies.append(pltpu.make_async_copy(self.cv_hbm.at[pid], self.vbuf.at[slot, p],
                                                self.sem.at[1, slot]))
        return copies

    def prime(self):
        for g in range(self.n_slots - 1):
            for cp in self._copies(g, g):
                cp.start()

    def begin(self, g):
        ahead = g + self.n_slots - 1

        @pl.when(ahead < self.n_total)
        def _():
            for cp in self._copies(ahead, ahead % self.n_slots):
                cp.start()

        for cp in self._copies(g, g % self.n_slots):
            cp.wait()

        @pl.when(g % self.n_chunks == 0)
        def _():
            q_row = self.q2_ref[g // self.n_chunks]
            self.qcol[...] = jnp.broadcast_to(q_row, (self.page, q_row.shape[1])).T
            self.dacc[...] = jnp.zeros(self.dacc.shape, F32)
            self.dlater[...] = jnp.zeros(self.dlater.shape, F32)

    def compute(self, g):
        n_heads, hd, page, ch = self.n_heads, self.hd, self.page, self.ch
        kbuf, vbuf, dacc, dlater, qcol = self.kbuf, self.vbuf, self.dacc, self.dlater, self.qcol
        slot = g % self.n_slots
        sub = lax.broadcasted_iota(jnp.int32, (n_heads, page), 0)
        zs = []
        for p in range(ch):
            zp = jnp.zeros((n_heads, page), F32)
            for h in range(n_heads):
                prod = kbuf[slot, p, h] * qcol[h * hd:(h + 1) * hd, :]
                zp = jnp.where(sub == h, jnp.sum(prod, axis=0, keepdims=True), zp)
            zs.append(zp)
        z = jnp.concatenate(zs, axis=0) + self.dbias_ref[...] * LOG2E
        sp = _softplus2(z)
        hi, lo = _split_bf16(sp)
        cum = jnp.dot(jnp.concatenate([hi, lo], axis=1), self.tt, preferred_element_type=F32)
        later = dlater[...]
        laters = [None] * ch
        for p in range(ch - 1, -1, -1):
            laters[p] = later
            later = later + cum[p * n_heads:(p + 1) * n_heads, page:]
        dlater[...] = later
        w = jnp.exp2((z - sp) - (jnp.concatenate(laters, axis=0) + cum[:, :page]))
        for h in range(n_heads):
            rows = slice(h * hd, (h + 1) * hd)
            a = dacc[rows, :]
            for p in range(ch):
                r = p * n_heads + h
                a = a + w[r:r + 1, :] * vbuf[slot, p, h]
            dacc[rows, :] = a

    def end(self, g):
        @pl.when(g % self.n_chunks == self.n_chunks - 1)
        def _():
            hi_a, lo_a = _split_bf16(self.dacc[...])
            ones_row = jnp.ones((V7X_SUBLANES, self.page), BF16)
            out = (lax.dot_general(ones_row, hi_a, _NT, preferred_element_type=F32)
                   + lax.dot_general(ones_row, lo_a, _NT, preferred_element_type=F32))
            self.o2_ref[g // self.n_chunks] = out[0:1]


def _attention_kernel(n_pages, seqs_per_step, pt_ref, bias_ref, q_ref, k_ref, v_ref, tt_ref,
                      q2_ref, dbias_ref, ck_hbm, cv_hbm, o_ref, o2_ref,
                      zbuf, lbbuf, hlbuf, wbuf, laterbuf, kbuf, vbuf, dacc, dlater, qcol, sem):
    hp = pl.program_id(1)
    seq_len, lanes = q_ref.shape
    tq, ck, tk = ATT_TQ, ATT_CK, ATT_TK
    nq = seq_len // tq
    nkb = ck // tk
    first_head = lax.broadcasted_iota(jnp.int32, (tq, lanes), 1) < HEAD_DIM
    bias = (bias_ref[2 * hp] * LOG2E, bias_ref[2 * hp + 1] * LOG2E)
    tt = tt_ref[...]
    causal = (lax.broadcasted_iota(jnp.int32, (tq, ck), 1)
              < lax.broadcasted_iota(jnp.int32, (tq, ck), 0))

    @pl.when(jnp.logical_and(pl.program_id(0) == 0, hp == 0))
    def _():
        zbuf[...] = jnp.zeros(zbuf.shape, zbuf.dtype)
        lbbuf[...] = jnp.zeros(lbbuf.shape, lbbuf.dtype)
        hlbuf[...] = jnp.zeros(hlbuf.shape, hlbuf.dtype)
        wbuf[...] = jnp.zeros(wbuf.shape, wbuf.dtype)
        laterbuf[...] = jnp.zeros(laterbuf.shape, laterbuf.dtype)

    o_ref[...] = jnp.zeros(o_ref.shape, o_ref.dtype)

    dec = _PagedAttention(n_pages, pt_ref, q2_ref, dbias_ref, tt, ck_hbm, cv_hbm, o2_ref,
                          kbuf, vbuf, dacc, dlater, qcol, sem)
    step_id = pl.program_id(0) * pl.num_programs(1) + hp
    chunks_per_step = seqs_per_step * dec.n_chunks

    @pl.when(step_id == 0)
    def _():
        dec.prime()

    def rows_of(qi):
        return pl.ds(pl.multiple_of(qi * tq, tq), tq)

    def keys_of(kj):
        return pl.ds(pl.multiple_of(kj * ck, ck), ck)

    def stage_qk(slot, qi, kj):
        q = q_ref[rows_of(qi), :]
        k = k_ref[keys_of(kj), :]
        zero = jnp.zeros_like(q)
        for h in range(2):
            qh = jnp.where(first_head, q, zero) if h == 0 else jnp.where(first_head, zero, q)
            zbuf[slot, h] = lax.dot_general(qh, k, _NT, preferred_element_type=F32) + bias[h]

    def stage_softplus(slot, masked):
        for h in range(2):
            z = zbuf[slot, h]
            sp = _softplus2(z)
            if masked:
                sp = jnp.where(causal, sp, 0.0)
            lbbuf[slot, h] = z - sp
            for kb in range(nkb):
                hi, lo = _split_bf16(sp[:, kb * tk:(kb + 1) * tk])
                hlbuf[slot, h, kb] = jnp.concatenate([hi, lo], axis=1)

    def stage_weights(slot, qi, valid, masked):
        rows = rows_of(qi)
        for h in range(2):
            old = laterbuf[h, rows, :]
            lat = jnp.zeros_like(old) if masked else old
            for kb in range(nkb - 1, -1, -1):
                cum = jnp.dot(hlbuf[slot, h, kb], tt, preferred_element_type=F32)
                cols = slice(kb * tk, (kb + 1) * tk)
                w = jnp.exp2(lbbuf[slot, h, :, cols] - (lat + cum[:, :tk]))
                if masked:
                    w = jnp.where(causal[:, cols], w, 0.0)
                wbuf[slot, h, :, cols] = w.astype(BF16)
                lat = lat + cum[:, tk:]
            laterbuf[h, rows, :] = jnp.where(valid, lat, old)

    def stage_pv(slot, qi, kj, valid, masked):
        rows = rows_of(qi)
        v = v_ref[keys_of(kj), :]
        pv = [jnp.dot(wbuf[slot, h], v, preferred_element_type=F32) for h in range(2)]
        acc = jnp.where(valid, jnp.where(first_head, pv[0], pv[1]), 0.0)
        o_ref[rows, :] = o_ref[rows, :] + acc

    def n_trips(n_items):
        return (n_items + 3 + 1) // 2

    def run_pass(masked, n_items, first, advance, first_trip):
        def step(i, parity, coords):
            c1, c2, c3, c4 = coords

            def valid(s):
                return jnp.logical_and(i - s >= 0, i - s < n_items)

            stage_pv((parity + 3) % 2, c4[0], c4[1], valid(3), masked)
            stage_weights((parity + 2) % 2, c3[0], valid(2), masked)
            stage_softplus((parity + 1) % 2, masked)
            stage_qk(parity, c1[0], c1[1])
            nxt = advance(*c1)
            more = i + 1 < n_items
            c1n = (jnp.where(more, nxt[0], c1[0]), jnp.where(more, nxt[1], c1[1]))
            return c1n, c1, c2, c3

        def two_steps(i2, coords):
            return step(2 * i2 + 1, 1, step(2 * i2, 0, coords))

        def two_steps_and_chunk(i2, coords):
            g = step_id * chunks_per_step + first_trip + i2
            coords = two_steps(i2, coords)
            dec.begin(g)
            dec.compute(g)
            dec.end(g)
            return coords

        n_with = max(0, min(n_trips(n_items), chunks_per_step - first_trip))
        coords = lax.fori_loop(0, n_with, two_steps_and_chunk, (first,) * 4)
        lax.fori_loop(n_with, n_trips(n_items), two_steps, coords)

    def next_diagonal(qi, kj):
        return qi + 1, kj + 1

    def next_below(qi, kj):
        wrap = kj == 0
        return jnp.where(wrap, qi + 1, qi), jnp.where(wrap, qi, kj - 1)

    n_below = nq * (nq - 1) // 2
    assert n_trips(nq) + n_trips(n_below) >= chunks_per_step
    run_pass(True, nq, (jnp.int32(0), jnp.int32(0)), next_diagonal, 0)
    run_pass(False, n_below, (jnp.int32(1), jnp.int32(0)), next_below, n_trips(nq))


def _post_kernel(x_ref, attn_ref, cmix_ref, p_ref, gattn_ref, wout_ref, gffn_ref, wg_ref, wu_ref,
                 wd_ref, gple_ref, wpg_ref, wpp_ref, o_ref):
    att_w = attn_ref.shape[1]
    mix_a = _rms(attn_ref[...], gattn_ref[...]).astype(BF16)
    h = (x_ref[...]
         + jnp.dot(mix_a, wout_ref[0:att_w, :], preferred_element_type=F32)
         + jnp.dot(cmix_ref[...], wout_ref[att_w:, :], preferred_element_type=F32))
    hn = _rms(h, gffn_ref[...]).astype(BF16)
    g = jnp.dot(hn, wg_ref[...], preferred_element_type=F32)
    u = jnp.dot(hn, wu_ref[...], preferred_element_type=F32)
    act = (g * jax.nn.sigmoid(g) * u).astype(BF16)
    h = h + jnp.dot(act, wd_ref[...], preferred_element_type=F32)
    hp = _rms(h, gple_ref[...]).astype(BF16)
    gate = jax.nn.sigmoid(jnp.dot(hp, wpg_ref[...], preferred_element_type=F32))
    proj = jnp.dot(p_ref[...].astype(BF16), wpp_ref[...], preferred_element_type=F32)
    o_ref[...] = h + proj * gate


def _resident(shape):
    return pl.BlockSpec(shape, lambda *_: (0,) * len(shape), pipeline_mode=pl.Buffered(1))


def _whole(shape):
    return pl.BlockSpec(shape, lambda *_: (0,) * len(shape))


def _row_spec(tm, width):
    return pl.BlockSpec((tm, width), lambda i: (i, 0))


def _params(*semantics):
    return pltpu.CompilerParams(dimension_semantics=semantics, vmem_limit_bytes=VMEM_LIMIT_BYTES)


def _inproj_weight_specs(d_model, w_total, att_w):
    return [_resident((1, d_model)), _resident((d_model, w_total)), _resident((1, att_w)),
            _resident((1, att_w)), _resident((V7X_MXU_DIM, V7X_MXU_DIM)),
            _resident((3, att_w)), _resident((1, att_w))]


def _inproj_prompt(x, seq_len, weights):
    m, d_model = x.shape
    w_total = weights[1].shape[1]
    att_w = weights[2].shape[1]
    tm = ROW_TILE_IN
    tiles_per_seq = seq_len // tm
    n_seq = m // seq_len
    rows = lambda dt: jax.ShapeDtypeStruct((m, att_w), dt)
    row_spec = _row_spec(tm, att_w)
    t_shape = jax.ShapeDtypeStruct((n_seq, att_w, seq_len), F32)
    t_spec = pl.BlockSpec((1, att_w, tm), lambda i: (i // tiles_per_seq, 0, i % tiles_per_seq))
    return pl.pallas_call(
        functools.partial(_inproj_prompt_kernel, tiles_per_seq),
        grid=(m // tm,),
        in_specs=[_row_spec(tm, d_model)] + _inproj_weight_specs(d_model, w_total, att_w),
        out_specs=[row_spec, t_spec, t_spec, row_spec, row_spec, row_spec,
                   pl.BlockSpec((1, V7X_SUBLANES, att_w), lambda i: (i // tiles_per_seq, 0, 0))],
        out_shape=[rows(BF16), t_shape, t_shape, rows(BF16), rows(BF16), rows(BF16),
                   jax.ShapeDtypeStruct((n_seq, V7X_SUBLANES, att_w), F32)],
        scratch_shapes=[pltpu.VMEM((tm + V7X_SUBLANES, att_w), F32)],
        compiler_params=_params("arbitrary"),
        name="inproj_prompt",
    )(x, *weights)


def _inproj_sample(x, s0, s1, weights):
    m, d_model = x.shape
    w_total = weights[1].shape[1]
    att_w = weights[2].shape[1]
    rows = lambda dt: jax.ShapeDtypeStruct((m, att_w), dt)
    return pl.pallas_call(
        _inproj_sample_kernel,
        grid=(1,),
        in_specs=[_row_spec(m, d_model)] + _inproj_weight_specs(d_model, w_total, att_w)
        + [_row_spec(m, att_w)] * 2,
        out_specs=[_row_spec(m, att_w)] * 5,
        out_shape=[rows(F32), rows(F32), rows(F32), rows(BF16), rows(F32)],
        compiler_params=_params("arbitrary"),
        name="inproj_sample",
    )(x, *weights, s0, s1)


def _cumsum_weights(tk):
    later = (jnp.arange(tk)[:, None] > jnp.arange(tk)[None, :]).astype(BF16)
    half = jnp.concatenate([later, jnp.ones((tk, tk), BF16)], axis=1)
    return jnp.concatenate([half, half], axis=0)


def _paged_operands(q2, att_bias, pool_k, pool_v, page_table):
    n, att_w = q2.shape
    _, n_heads, head_dim, page = pool_k.shape
    ch = DEC_CHUNK_PAGES
    assert page_table.shape[1] % ch == 0 and page == ATT_TK and n_heads == V7X_SUBLANES
    dbias = jnp.tile(att_bias.reshape(n_heads, 1), (ch, page))
    tt = _cumsum_weights(page)
    page_buf = pltpu.VMEM((DEC_SLOTS, ch, n_heads, head_dim, page), F32)
    return dict(
        prefetch=page_table.reshape(-1),
        operands=(q2.reshape(n, 1, att_w), dbias, tt, pool_k, pool_v),
        in_specs=[_whole((n, 1, att_w)), _whole(dbias.shape), _whole(tt.shape),
                  pl.BlockSpec(memory_space=pl.ANY), pl.BlockSpec(memory_space=pl.ANY)],
        out_spec=_whole((n, 1, att_w)),
        out_shape=jax.ShapeDtypeStruct((n, 1, att_w), F32),
        scratch=[page_buf, page_buf,
                 pltpu.VMEM((att_w, page), F32),
                 pltpu.VMEM((n_heads, page), F32),
                 pltpu.VMEM((att_w, page), F32),
                 pltpu.SemaphoreType.DMA((2, DEC_SLOTS))])


def _attention(q, k, v, att_bias, paged, n_prompt, seq_len):
    m, att_w = q.shape
    tq, ck, tk = ATT_TQ, ATT_CK, ATT_TK
    pair_w = 2 * HEAD_DIM
    n_steps = n_prompt * (att_w // pair_w)
    n_sample = paged["out_shape"].shape[0]
    assert pair_w == V7X_LANES and tq == ck and ck % tk == 0 and seq_len % tq == 0
    assert n_sample % n_steps == 0
    n_pages = paged["prefetch"].shape[0] // n_sample
    q2, dbias, tt, pool_k, pool_v = paged["operands"]
    seq_block = pl.BlockSpec((seq_len, pair_w), lambda b, hp, pt: (b, hp))
    grid_spec = pltpu.PrefetchScalarGridSpec(
        num_scalar_prefetch=1,
        grid=(n_prompt, att_w // pair_w),
        in_specs=[pl.BlockSpec(memory_space=pltpu.SMEM), seq_block, seq_block, seq_block,
                  paged["in_specs"][2], paged["in_specs"][0], paged["in_specs"][1],
                  paged["in_specs"][3], paged["in_specs"][4]],
        out_specs=[seq_block, paged["out_spec"]],
        scratch_shapes=[pltpu.VMEM((2, 2, tq, ck), F32),
                        pltpu.VMEM((2, 2, tq, ck), F32),
                        pltpu.VMEM((2, 2, ck // tk, tq, 2 * tk), BF16),
                        pltpu.VMEM((2, 2, tq, ck), BF16),
                        pltpu.VMEM((2, seq_len, pair_w), F32)]
        + paged["scratch"],
    )
    attn, attn2 = pl.pallas_call(
        functools.partial(_attention_kernel, n_pages, n_sample // n_steps),
        grid_spec=grid_spec,
        out_shape=[jax.ShapeDtypeStruct((m, att_w), F32), paged["out_shape"]],
        compiler_params=_params("arbitrary", "arbitrary"),
        name="attention",
    )(paged["prefetch"], att_bias, q, k, v, tt, q2, dbias, pool_k, pool_v)
    return attn, attn2.reshape(n_sample, att_w)


def _post(x, attn, cmix, p, weights):
    m, d_model = x.shape
    tm = min(ROW_TILE_POST, m)
    specs = [_row_spec(tm, d_model), _row_spec(tm, attn.shape[1]), _row_spec(tm, cmix.shape[1]),
             _row_spec(tm, p.shape[1])] + [_resident(w.shape) for w in weights]
    return pl.pallas_call(
        _post_kernel,
        grid=(m // tm,),
        in_specs=specs,
        out_specs=_row_spec(tm, d_model),
        out_shape=jax.ShapeDtypeStruct((m, d_model), F32),
        compiler_params=_params("arbitrary"),
        name="post_mixer",
    )(x, attn, cmix, p, *weights)


def kernel(x_prompt, x_sample, cache_k, cache_v, state_conv, page_table, p_prompt, p_sample,
           norm_mix, w_in, q_norm, k_norm, att_bias, conv_w, attn_out_norm, conv_out_norm, w_out,
           norm_ffn, w_gate, w_up, w_down, ple_norm, w_ple_gate, w_ple_proj):
    bp, sp, d_model = x_prompt.shape
    bs, ts, _ = x_sample.shape
    depth = w_in.shape[0]
    att_w = w_in.shape[2] // 6
    n_heads = att_w // HEAD_DIM
    assert ts == 1 and conv_w.shape[1] == 3 and conv_w.shape[2] == att_w

    gidx = jnp.arange(V7X_MXU_DIM) // HEAD_DIM
    group_ones = (gidx[:, None] == gidx[None, :]).astype(BF16)
    row = lambda vec: vec.reshape(1, -1)

    hp = x_prompt.reshape(bp * sp, d_model)
    hs = x_sample.reshape(bs * ts, d_model)
    k_p, v_p, c_p, k_s, v_s, c_s = [], [], [], [], [], []
    for i in range(depth):
        in_w = (row(norm_mix[i]), w_in[i].astype(BF16), row(jnp.tile(q_norm[i], n_heads)),
                row(jnp.tile(k_norm[i], n_heads)), group_ones, conv_w[i], row(conv_out_norm[i]))
        post_w = (row(attn_out_norm[i]), w_out[i].astype(BF16), row(norm_ffn[i]),
                  w_gate[i].astype(BF16), w_up[i].astype(BF16), w_down[i].astype(BF16),
                  row(ple_norm[i]), w_ple_gate[i].astype(BF16), w_ple_proj[i].astype(BF16))

        s0 = state_conv[i][:, 0, :]
        s1 = state_conv[i][:, 1, :]
        q, kt, vt, kb, vb, cmix, tail = _inproj_prompt(hp, sp, in_w)
        q2, k2, v2, cmix2, u2 = _inproj_sample(hs, s0, s1, in_w)
        pool_k = jnp.transpose(cache_k[i], (0, 2, 3, 1))
        pool_v = jnp.transpose(cache_v[i], (0, 2, 3, 1))
        paged = _paged_operands(q2, att_bias[i], pool_k, pool_v, page_table)

        attn, attn2 = _attention(q, kb, vb, att_bias[i], paged, bp, sp)
        hp = _post(hp, attn, cmix, p_prompt[i].reshape(bp * sp, -1), post_w)
        from_t = lambda t: jnp.transpose(t.reshape(bp, n_heads, HEAD_DIM, sp), (0, 3, 1, 2))
        k_p.append(from_t(kt))
        v_p.append(from_t(vt))
        c_p.append(tail[:, V7X_SUBLANES - 2:, :])

        hs = _post(hs, attn2, cmix2, p_sample[i].reshape(bs * ts, -1), post_w)
        k_s.append(k2.reshape(bs, ts, n_heads, HEAD_DIM))
        v_s.append(v2.reshape(bs, ts, n_heads, HEAD_DIM))
        c_s.append(jnp.stack([s1, u2], axis=1))

    return (hp.reshape(bp, sp, d_model), hs.reshape(bs, ts, d_model),
            jnp.stack(k_p), jnp.stack(v_p), jnp.stack(c_p),
            jnp.stack(k_s), jnp.stack(v_s), jnp.stack(c_s))
```

```python
import functools

import jax
import jax.numpy as jnp
from jax import lax
from jax.experimental import pallas as pl
from jax.experimental.pallas import tpu as pltpu

F32 = jnp.float32
BF16 = jnp.bfloat16

EPS = 1e-6
HEAD_DIM = 64
LOG2E = 1.4426950408889634

V7X_LANES = 128
V7X_SUBLANES = 8
V7X_MXU_DIM = 256
V7X_VMEM_BYTES = 64 * 1024 * 1024
VMEM_LIMIT_BYTES = V7X_VMEM_BYTES * 7 // 8

ROW_TILE_IN = 512
ROW_TILE_POST = 256
ATT_TQ = 256
ATT_CK = 256
ATT_TK = V7X_LANES
DEC_CHUNK_PAGES = 8
DEC_SLOTS = 3

_NT = (((1,), (1,)), ((), ()))


def _rms(x, g):
    ms = jnp.mean(x * x, axis=-1, keepdims=True)
    return x * lax.rsqrt(ms + EPS) * g


def _softplus2(z2):
    return jnp.maximum(z2, 0.0) + jnp.log2(1.0 + jnp.exp2(-jnp.abs(z2)))


def _split_bf16(x):
    hi = x.astype(BF16)
    lo = (x - hi.astype(F32)).astype(BF16)
    return hi, lo


def _head_rms(t, gn, group_ones):
    sq = (t * t).astype(BF16)
    w = group_ones.shape[0]
    ss = jnp.concatenate(
        [jnp.dot(sq[:, i:i + w], group_ones, preferred_element_type=F32)
         for i in range(0, t.shape[1], w)], axis=1)
    return t * lax.rsqrt(ss * (1.0 / HEAD_DIM) + EPS) * gn


def _in_proj(x_ref, gmix_ref, win_ref, qn_ref, kn_ref, gones_ref):
    w = qn_ref.shape[1]
    hn = _rms(x_ref[...], gmix_ref[...]).astype(BF16)

    def sec(i):
        return jnp.dot(hn, win_ref[:, i * w:(i + 1) * w], preferred_element_type=F32)

    gones = gones_ref[...]
    q = _head_rms(sec(0), qn_ref[...], gones) * (HEAD_DIM ** -0.5 * LOG2E)
    k = _head_rms(sec(1), kn_ref[...], gones)
    v = sec(2)
    b_gate = sec(3)
    u = sec(4) * sec(5)
    return q, k, v, b_gate, u


def _inproj_prompt_kernel(tiles_per_seq, x_ref, gmix_ref, win_ref, qn_ref, kn_ref, gones_ref,
                          cw_ref, cn_ref,
                          q_ref, kt_ref, vt_ref, kb_ref, vb_ref, cmix_ref, tail_ref, ubuf):
    tm = x_ref.shape[0]
    pad = V7X_SUBLANES
    q, k, v, b_gate, u = _in_proj(x_ref, gmix_ref, win_ref, qn_ref, kn_ref, gones_ref)
    q_ref[...] = q.astype(BF16)
    kt_ref[0] = k.T
    vt_ref[0] = v.T
    kb_ref[...] = k.astype(BF16)
    vb_ref[...] = v.astype(BF16)

    @pl.when(pl.program_id(0) % tiles_per_seq == 0)
    def _():
        ubuf[0:pad, :] = jnp.zeros((pad, ubuf.shape[1]), F32)

    ubuf[pad:pad + tm, :] = u
    cw = cw_ref[...]
    y = (cw[0:1] * ubuf[pad - 2:pad - 2 + tm, :] + cw[1:2] * ubuf[pad - 1:pad - 1 + tm, :]
         + cw[2:3] * u)
    cmix_ref[...] = _rms(b_gate * y, cn_ref[...]).astype(BF16)
    tail = u[tm - pad:, :]
    ubuf[0:pad, :] = tail
    tail_ref[0] = tail


def _inproj_sample_kernel(x_ref, gmix_ref, win_ref, qn_ref, kn_ref, gones_ref, cw_ref, cn_ref,
                          s0_ref, s1_ref,
                          q_ref, k_ref, v_ref, cmix_ref, u_ref):
    q, k, v, b_gate, u = _in_proj(x_ref, gmix_ref, win_ref, qn_ref, kn_ref, gones_ref)
    q_ref[...] = q
    k_ref[...] = k
    v_ref[...] = v
    u_ref[...] = u
    cw = cw_ref[...]
    y = cw[0:1] * s0_ref[...] + cw[1:2] * s1_ref[...] + cw[2:3] * u
    cmix_ref[...] = _rms(b_gate * y, cn_ref[...]).astype(BF16)


class _PagedAttention:
    def __init__(self, n_pages, pt_ref, q2_ref, dbias_ref, tt, ck_hbm, cv_hbm, o2_ref,
                 kbuf, vbuf, dacc, dlater, qcol, sem):
        self.n_pages, self.pt_ref, self.q2_ref, self.dbias_ref, self.tt = n_pages, pt_ref, q2_ref, dbias_ref, tt
        self.ck_hbm, self.cv_hbm, self.o2_ref = ck_hbm, cv_hbm, o2_ref
        self.kbuf, self.vbuf, self.dacc, self.dlater, self.qcol, self.sem = kbuf, vbuf, dacc, dlater, qcol, sem
        _, self.n_heads, self.hd, self.page = ck_hbm.shape
        self.n_slots, self.ch = kbuf.shape[:2]
        self.n_chunks = n_pages // self.ch
        self.n_total = o2_ref.shape[0] * self.n_chunks

    def _copies(self, g, slot):
        seq = g // self.n_chunks
        c = g % self.n_chunks
        base = seq * self.n_pages + (self.n_chunks - 1 - c) * self.ch
        copies = []
        for p in range(self.ch):
            pid = self.pt_ref[base + p]
            copies.append(pltpu.make_async_copy(self.ck_hbm.at[pid], self.kbuf.at[slot, p],
                                                self.sem.at[0, slot]))
            copies.append(pltpu.make_async_copy(self.cv_hbm.at[pid], self.vbuf.at[slot, p],
                                                self.sem.at[1, slot]))
        return copies

    def prime(self):
        for g in range(self.n_slots - 1):
            for cp in self._copies(g, g):
                cp.start()

    def begin(self, g):
        ahead = g + self.n_slots - 1

        @pl.when(ahead < self.n_total)
        def _():
            for cp in self._copies(ahead, ahead % self.n_slots):
                cp.start()

        for cp in self._copies(g, g % self.n_slots):
            cp.wait()

        @pl.when(g % self.n_chunks == 0)
        def _():
            q_row = self.q2_ref[g // self.n_chunks]
            self.qcol[...] = jnp.broadcast_to(q_row, (self.page, q_row.shape[1])).T
            self.dacc[...] = jnp.zeros(self.dacc.shape, F32)
            self.dlater[...] = jnp.zeros(self.dlater.shape, F32)

    def compute(self, g):
        n_heads, hd, page, ch = self.n_heads, self.hd, self.page, self.ch
        kbuf, vbuf, dacc, dlater, qcol = self.kbuf, self.vbuf, self.dacc, self.dlater, self.qcol
        slot = g % self.n_slots
        sub = lax.broadcasted_iota(jnp.int32, (n_heads, page), 0)
        zs = []
        for p in range(ch):
            zp = jnp.zeros((n_heads, page), F32)
            for h in range(n_heads):
                prod = kbuf[slot, p, h] * qcol[h * hd:(h + 1) * hd, :]
                zp = jnp.where(sub == h, jnp.sum(prod, axis=0, keepdims=True), zp)
            zs.append(zp)
        z = jnp.concatenate(zs, axis=0) + self.dbias_ref[...] * LOG2E
        sp = _softplus2(z)
        hi, lo = _split_bf16(sp)
        cum = jnp.dot(jnp.concatenate([hi, lo], axis=1), self.tt, preferred_element_type=F32)
        later = dlater[...]
        laters = [None] * ch
        for p in range(ch - 1, -1, -1):
            laters[p] = later
            later = later + cum[p * n_heads:(p + 1) * n_heads, page:]
        dlater[...] = later
        w = jnp.exp2((z - sp) - (jnp.concatenate(laters, axis=0) + cum[:, :page]))
        for h in range(n_heads):
            rows = slice(h * hd, (h + 1) * hd)
            a = dacc[rows, :]
            for p in range(ch):
                r = p * n_heads + h
                a = a + w[r:r + 1, :] * vbuf[slot, p, h]
            dacc[rows, :] = a

    def end(self, g):
        @pl.when(g % self.n_chunks == self.n_chunks - 1)
        def _():
            hi_a, lo_a = _split_bf16(self.dacc[...])
            ones_row = jnp.ones((V7X_SUBLANES, self.page), BF16)
            out = (lax.dot_general(ones_row, hi_a, _NT, preferred_element_type=F32)
                   + lax.dot_general(ones_row, lo_a, _NT, preferred_element_type=F32))
            self.o2_ref[g // self.n_chunks] = out[0:1]


def _attention_kernel(n_pages, seqs_per_step, pt_ref, bias_ref, q_ref, k_ref, v_ref, tt_ref,
                      q2_ref, dbias_ref, ck_hbm, cv_hbm, o_ref, o2_ref,
                      zbuf, lbbuf, hlbuf, wbuf, laterbuf, kbuf, vbuf, dacc, dlater, qcol, sem):
    hp = pl.program_id(1)
    seq_len, lanes = q_ref.shape
    tq, ck, tk = ATT_TQ, ATT_CK, ATT_TK
    nq = seq_len // tq
    nkb = ck // tk
    first_head = lax.broadcasted_iota(jnp.int32, (tq, lanes), 1) < HEAD_DIM
    bias = (bias_ref[2 * hp] * LOG2E, bias_ref[2 * hp + 1] * LOG2E)
    tt = tt_ref[...]
    causal = (lax.broadcasted_iota(jnp.int32, (tq, ck), 1)
              < lax.broadcasted_iota(jnp.int32, (tq, ck), 0))

    @pl.when(jnp.logical_and(pl.program_id(0) == 0, hp == 0))
    def _():
        zbuf[...] = jnp.zeros(zbuf.shape, zbuf.dtype)
        lbbuf[...] = jnp.zeros(lbbuf.shape, lbbuf.dtype)
        hlbuf[...] = jnp.zeros(hlbuf.shape, hlbuf.dtype)
        wbuf[...] = jnp.zeros(wbuf.shape, wbuf.dtype)
        laterbuf[...] = jnp.zeros(laterbuf.shape, laterbuf.dtype)

    o_ref[...] = jnp.zeros(o_ref.shape, o_ref.dtype)

    dec = _PagedAttention(n_pages, pt_ref, q2_ref, dbias_ref, tt, ck_hbm, cv_hbm, o2_ref,
                          kbuf, vbuf, dacc, dlater, qcol, sem)
    step_id = pl.program_id(0) * pl.num_programs(1) + hp
    chunks_per_step = seqs_per_step * dec.n_chunks

    @pl.when(step_id == 0)
    def _():
        dec.prime()

    def rows_of(qi):
        return pl.ds(pl.multiple_of(qi * tq, tq), tq)

    def keys_of(kj):
        return pl.ds(pl.multiple_of(kj * ck, ck), ck)

    def stage_qk(slot, qi, kj):
        q = q_ref[rows_of(qi), :]
        k = k_ref[keys_of(kj), :]
        zero = jnp.zeros_like(q)
        for h in range(2):
            qh = jnp.where(first_head, q, zero) if h == 0 else jnp.where(first_head, zero, q)
            zbuf[slot, h] = lax.dot_general(qh, k, _NT, preferred_element_type=F32) + bias[h]

    def stage_softplus(slot, masked):
        for h in range(2):
            z = zbuf[slot, h]
            sp = _softplus2(z)
            if masked:
                sp = jnp.where(causal, sp, 0.0)
            lbbuf[slot, h] = z - sp
            for kb in range(nkb):
                hi, lo = _split_bf16(sp[:, kb * tk:(kb + 1) * tk])
                hlbuf[slot, h, kb] = jnp.concatenate([hi, lo], axis=1)

    def stage_weights(slot, qi, valid, masked):
        rows = rows_of(qi)
        for h in range(2):
            old = laterbuf[h, rows, :]
            lat = jnp.zeros_like(old) if masked else old
            for kb in range(nkb - 1, -1, -1):
                cum = jnp.dot(hlbuf[slot, h, kb], tt, preferred_element_type=F32)
                cols = slice(kb * tk, (kb + 1) * tk)
                w = jnp.exp2(lbbuf[slot, h, :, cols] - (lat + cum[:, :tk]))
                if masked:
                    w = jnp.where(causal[:, cols], w, 0.0)
                wbuf[slot, h, :, cols] = w.astype(BF16)
                lat = lat + cum[:, tk:]
            laterbuf[h, rows, :] = jnp.where(valid, lat, old)

    def stage_pv(slot, qi, kj, valid, masked):
        rows = rows_of(qi)
        v = v_ref[keys_of(kj), :]
        pv = [jnp.dot(wbuf[slot, h], v, preferred_element_type=F32) for h in range(2)]
        acc = jnp.where(valid, jnp.where(first_head, pv[0], pv[1]), 0.0)
        o_ref[rows, :] = o_ref[rows, :] + acc

    def n_trips(n_items):
        return (n_items + 3 + 1) // 2

    def run_pass(masked, n_items, first, advance, first_trip):
        def step(i, parity, coords):
            c1, c2, c3, c4 = coords

            def valid(s):
                return jnp.logical_and(i - s >= 0, i - s < n_items)

            stage_pv((parity + 3) % 2, c4[0], c4[1], valid(3), masked)
            stage_weights((parity + 2) % 2, c3[0], valid(2), masked)
            stage_softplus((parity + 1) % 2, masked)
            stage_qk(parity, c1[0], c1[1])
            nxt = advance(*c1)
            more = i + 1 < n_items
            c1n = (jnp.where(more, nxt[0], c1[0]), jnp.where(more, nxt[1], c1[1]))
            return c1n, c1, c2, c3

        def two_steps(i2, coords):
            return step(2 * i2 + 1, 1, step(2 * i2, 0, coords))

        def two_steps_and_chunk(i2, coords):
            g = step_id * chunks_per_step + first_trip + i2
            dec.begin(g)
            coords = two_steps(i2, coords)
            dec.compute(g)
            dec.end(g)
            return coords

        n_with = max(0, min(n_trips(n_items), chunks_per_step - first_trip))
        coords = lax.fori_loop(0, n_with, two_steps_and_chunk, (first,) * 4)
        lax.fori_loop(n_with, n_trips(n_items), two_steps, coords)

    def next_diagonal(qi, kj):
        return qi + 1, kj + 1

    def next_below(qi, kj):
        wrap = kj == 0
        return jnp.where(wrap, qi + 1, qi), jnp.where(wrap, qi, kj - 1)

    n_below = nq * (nq - 1) // 2
    assert n_trips(nq) + n_trips(n_below) >= chunks_per_step
    run_pass(True, nq, (jnp.int32(0), jnp.int32(0)), next_diagonal, 0)
    run_pass(False, n_below, (jnp.int32(1), jnp.int32(0)), next_below, n_trips(nq))


def _post_kernel(x_ref, attn_ref, cmix_ref, p_ref, gattn_ref, wout_ref, gffn_ref, wg_ref, wu_ref,
                 wd_ref, gple_ref, wpg_ref, wpp_ref, o_ref):
    att_w = attn_ref.shape[1]
    mix_a = _rms(attn_ref[...], gattn_ref[...]).astype(BF16)
    h = (x_ref[...]
         + jnp.dot(mix_a, wout_ref[0:att_w, :], preferred_element_type=F32)
         + jnp.dot(cmix_ref[...], wout_ref[att_w:, :], preferred_element_type=F32))
    hn = _rms(h, gffn_ref[...]).astype(BF16)
    g = jnp.dot(hn, wg_ref[...], preferred_element_type=F32)
    u = jnp.dot(hn, wu_ref[...], preferred_element_type=F32)
    act = (g * jax.nn.sigmoid(g) * u).astype(BF16)
    h = h + jnp.dot(act, wd_ref[...], preferred_element_type=F32)
    hp = _rms(h, gple_ref[...]).astype(BF16)
    gate = jax.nn.sigmoid(jnp.dot(hp, wpg_ref[...], preferred_element_type=F32))
    proj = jnp.dot(p_ref[...].astype(BF16), wpp_ref[...], preferred_element_type=F32)
    o_ref[...] = h + proj * gate


def _resident(shape):
    return pl.BlockSpec(shape, lambda *_: (0,) * len(shape), pipeline_mode=pl.Buffered(1))


def _whole(shape):
    return pl.BlockSpec(shape, lambda *_: (0,) * len(shape))


def _row_spec(tm, width):
    return pl.BlockSpec((tm, width), lambda i: (i, 0))


def _params(*semantics):
    return pltpu.CompilerParams(dimension_semantics=semantics, vmem_limit_bytes=VMEM_LIMIT_BYTES)


def _inproj_weight_specs(d_model, w_total, att_w):
    return [_resident((1, d_model)), _resident((d_model, w_total)), _resident((1, att_w)),
            _resident((1, att_w)), _resident((V7X_MXU_DIM, V7X_MXU_DIM)),
            _resident((3, att_w)), _resident((1, att_w))]


def _inproj_prompt(x, seq_len, weights):
    m, d_model = x.shape
    w_total = weights[1].shape[1]
    att_w = weights[2].shape[1]
    tm = ROW_TILE_IN
    tiles_per_seq = seq_len // tm
    n_seq = m // seq_len
    rows = lambda dt: jax.ShapeDtypeStruct((m, att_w), dt)
    row_spec = _row_spec(tm, att_w)
    t_shape = jax.ShapeDtypeStruct((n_seq, att_w, seq_len), F32)
    t_spec = pl.BlockSpec((1, att_w, tm), lambda i: (i // tiles_per_seq, 0, i % tiles_per_seq))
    return pl.pallas_call(
        functools.partial(_inproj_prompt_kernel, tiles_per_seq),
        grid=(m // tm,),
        in_specs=[_row_spec(tm, d_model)] + _inproj_weight_specs(d_model, w_total, att_w),
        out_specs=[row_spec, t_spec, t_spec, row_spec, row_spec, row_spec,
                   pl.BlockSpec((1, V7X_SUBLANES, att_w), lambda i: (i // tiles_per_seq, 0, 0))],
        out_shape=[rows(BF16), t_shape, t_shape, rows(BF16), rows(BF16), rows(BF16),
                   jax.ShapeDtypeStruct((n_seq, V7X_SUBLANES, att_w), F32)],
        scratch_shapes=[pltpu.VMEM((tm + V7X_SUBLANES, att_w), F32)],
        compiler_params=_params("arbitrary"),
        name="inproj_prompt",
    )(x, *weights)


def _inproj_sample(x, s0, s1, weights):
    m, d_model = x.shape
    w_total = weights[1].shape[1]
    att_w = weights[2].shape[1]
    rows = lambda dt: jax.ShapeDtypeStruct((m, att_w), dt)
    return pl.pallas_call(
        _inproj_sample_kernel,
        grid=(1,),
        in_specs=[_row_spec(m, d_model)] + _inproj_weight_specs(d_model, w_total, att_w)
        + [_row_spec(m, att_w)] * 2,
        out_specs=[_row_spec(m, att_w)] * 5,
        out_shape=[rows(F32), rows(F32), rows(F32), rows(BF16), rows(F32)],
        compiler_params=_params("arbitrary"),
        name="inproj_sample",
    )(x, *weights, s0, s1)


def _cumsum_weights(tk):
    later = (jnp.arange(tk)[:, None] > jnp.arange(tk)[None, :]).astype(BF16)
    half = jnp.concatenate([later, jnp.ones((tk, tk), BF16)], axis=1)
    return jnp.concatenate([half, half], axis=0)


def _paged_operands(q2, att_bias, pool_k, pool_v, page_table):
    n, att_w = q2.shape
    _, n_heads, head_dim, page = pool_k.shape
    ch = DEC_CHUNK_PAGES
    assert page_table.shape[1] % ch == 0 and page == ATT_TK and n_heads == V7X_SUBLANES
    dbias = jnp.tile(att_bias.reshape(n_heads, 1), (ch, page))
    tt = _cumsum_weights(page)
    page_buf = pltpu.VMEM((DEC_SLOTS, ch, n_heads, head_dim, page), F32)
    return dict(
        prefetch=page_table.reshape(-1),
        operands=(q2.reshape(n, 1, att_w), dbias, tt, pool_k, pool_v),
        in_specs=[_whole((n, 1, att_w)), _whole(dbias.shape), _whole(tt.shape),
                  pl.BlockSpec(memory_space=pl.ANY), pl.BlockSpec(memory_space=pl.ANY)],
        out_spec=_whole((n, 1, att_w)),
        out_shape=jax.ShapeDtypeStruct((n, 1, att_w), F32),
        scratch=[page_buf, page_buf,
                 pltpu.VMEM((att_w, page), F32),
                 pltpu.VMEM((n_heads, page), F32),
                 pltpu.VMEM((att_w, page), F32),
                 pltpu.SemaphoreType.DMA((2, DEC_SLOTS))])


def _attention(q, k, v, att_bias, paged, n_prompt, seq_len):
    m, att_w = q.shape
    tq, ck, tk = ATT_TQ, ATT_CK, ATT_TK
    pair_w = 2 * HEAD_DIM
    n_steps = n_prompt * (att_w // pair_w)
    n_sample = paged["out_shape"].shape[0]
    assert pair_w == V7X_LANES and tq == ck and ck % tk == 0 and seq_len % tq == 0
    assert n_sample % n_steps == 0
    n_pages = paged["prefetch"].shape[0] // n_sample
    q2, dbias, tt, pool_k, pool_v = paged["operands"]
    seq_block = pl.BlockSpec((seq_len, pair_w), lambda b, hp, pt: (b, hp))
    grid_spec = pltpu.PrefetchScalarGridSpec(
        num_scalar_prefetch=1,
        grid=(n_prompt, att_w // pair_w),
        in_specs=[pl.BlockSpec(memory_space=pltpu.SMEM), seq_block, seq_block, seq_block,
                  paged["in_specs"][2], paged["in_specs"][0], paged["in_specs"][1],
                  paged["in_specs"][3], paged["in_specs"][4]],
        out_specs=[seq_block, paged["out_spec"]],
        scratch_shapes=[pltpu.VMEM((2, 2, tq, ck), F32),
                        pltpu.VMEM((2, 2, tq, ck), F32),
                        pltpu.VMEM((2, 2, ck // tk, tq, 2 * tk), BF16),
                        pltpu.VMEM((2, 2, tq, ck), BF16),
                        pltpu.VMEM((2, seq_len, pair_w), F32)]
        + paged["scratch"],
    )
    attn, attn2 = pl.pallas_call(
        functools.partial(_attention_kernel, n_pages, n_sample // n_steps),
        grid_spec=grid_spec,
        out_shape=[jax.ShapeDtypeStruct((m, att_w), F32), paged["out_shape"]],
        compiler_params=_params("arbitrary", "arbitrary"),
        name="attention",
    )(paged["prefetch"], att_bias, q, k, v, tt, q2, dbias, pool_k, pool_v)
    return attn, attn2.reshape(n_sample, att_w)


def _post(x, attn, cmix, p, weights):
    m, d_model = x.shape
    tm = min(ROW_TILE_POST, m)
    specs = [_row_spec(tm, d_model), _row_spec(tm, attn.shape[1]), _row_spec(tm, cmix.shape[1]),
             _row_spec(tm, p.shape[1])] + [_resident(w.shape) for w in weights]
    return pl.pallas_call(
        _post_kernel,
        grid=(m // tm,),
        in_specs=specs,
        out_specs=_row_spec(tm, d_model),
        out_shape=jax.ShapeDtypeStruct((m, d_model), F32),
        compiler_params=_params("arbitrary"),
        name="post_mixer",
    )(x, attn, cmix, p, *weights)


def kernel(x_prompt, x_sample, cache_k, cache_v, state_conv, page_table, p_prompt, p_sample,
           norm_mix, w_in, q_norm, k_norm, att_bias, conv_w, attn_out_norm, conv_out_norm, w_out,
           norm_ffn, w_gate, w_up, w_down, ple_norm, w_ple_gate, w_ple_proj):
    bp, sp, d_model = x_prompt.shape
    bs, ts, _ = x_sample.shape
    depth = w_in.shape[0]
    att_w = w_in.shape[2] // 6
    n_heads = att_w // HEAD_DIM
    assert ts == 1 and conv_w.shape[1] == 3 and conv_w.shape[2] == att_w

    gidx = jnp.arange(V7X_MXU_DIM) // HEAD_DIM
    group_ones = (gidx[:, None] == gidx[None, :]).astype(BF16)
    row = lambda vec: vec.reshape(1, -1)

    hp = x_prompt.reshape(bp * sp, d_model)
    hs = x_sample.reshape(bs * ts, d_model)
    k_p, v_p, c_p, k_s, v_s, c_s = [], [], [], [], [], []
    for i in range(depth):
        in_w = (row(norm_mix[i]), w_in[i].astype(BF16), row(jnp.tile(q_norm[i], n_heads)),
                row(jnp.tile(k_norm[i], n_heads)), group_ones, conv_w[i], row(conv_out_norm[i]))
        post_w = (row(attn_out_norm[i]), w_out[i].astype(BF16), row(norm_ffn[i]),
                  w_gate[i].astype(BF16), w_up[i].astype(BF16), w_down[i].astype(BF16),
                  row(ple_norm[i]), w_ple_gate[i].astype(BF16), w_ple_proj[i].astype(BF16))

        s0 = state_conv[i][:, 0, :]
        s1 = state_conv[i][:, 1, :]
        q, kt, vt, kb, vb, cmix, tail = _inproj_prompt(hp, sp, in_w)
        q2, k2, v2, cmix2, u2 = _inproj_sample(hs, s0, s1, in_w)
        pool_k = jnp.transpose(cache_k[i], (0, 2, 3, 1))
        pool_v = jnp.transpose(cache_v[i], (0, 2, 3, 1))
        paged = _paged_operands(q2, att_bias[i], pool_k, pool_v, page_table)

        attn, attn2 = _attention(q, kb, vb, att_bias[i], paged, bp, sp)
        hp = _post(hp, attn, cmix, p_prompt[i].reshape(bp * sp, -1), post_w)
        from_t = lambda t: jnp.transpose(t.reshape(bp, n_heads, HEAD_DIM, sp), (0, 3, 1, 2))
        k_p.append(from_t(kt))
        v_p.append(from_t(vt))
        c_p.append(tail[:, V7X_SUBLANES - 2:, :])

        hs = _post(hs, attn2, cmix2, p_sample[i].reshape(bs * ts, -1), post_w)
        k_s.append(k2.reshape(bs, ts, n_heads, HEAD_DIM))
        v_s.append(v2.reshape(bs, ts, n_heads, HEAD_DIM))
        c_s.append(jnp.stack([s1, u2], axis=1))

    return (hp.reshape(bp, sp, d_model), hs.reshape(bs, ts, d_model),
            jnp.stack(k_p), jnp.stack(v_p), jnp.stack(c_p),
            jnp.stack(k_s), jnp.stack(v_s), jnp.stack(c_s))
```
